```python
import math
import jax, jax.numpy as jnp
from jax import lax
import numpy as np

D_MODEL = 1024
BATCH = 32
SEQ = 2048
DEPTH = 1

CHUNK = 64
MIX_WIDTH = D_MODEL
ATT_WIDTH = MIX_WIDTH // 2
GMLP_WIDTH = MIX_WIDTH - ATT_WIDTH
ATT_HEAD_DIM = 64
ATT_VDIM = 2 * ATT_HEAD_DIM
ATT_HEADS = ATT_WIDTH // ATT_VDIM
Q_BLOCK = 128
GMLP_BLOCK = 128
GMLP_GROUPS = 4
GMLP_GROUP_DIM = GMLP_WIDTH // GMLP_GROUPS
IN_WIDTH = 3 * ATT_WIDTH + 2 * GMLP_WIDTH
PEER_HEADS = 8
PEER_KEYS = 128
PEER_EXPERTS = PEER_KEYS * PEER_KEYS
PEER_TOPK = 16
PEER_QDIM = 256
PEER_HALF = PEER_QDIM // 2
PEER_TOKEN_BLOCK = 128
EPS = 1e-6

kernel_name = 'hybrid_diffattn_gmlp_peer_block'


def rmsnorm(x, g):
    xf = x.astype(jnp.float32)
    y = xf * lax.rsqrt(jnp.mean(xf * xf, axis=-1, keepdims=True) + EPS)
    return (y * g.astype(jnp.float32)).astype(x.dtype)


def alibi_slopes(n):
    return 2.0 ** (-8.0 * jnp.arange(1, n + 1, dtype=jnp.float32) / n)


def diff_attention(q, k, v, lam, slopes):
    B, S = q.shape[0], q.shape[1]
    nb = S // Q_BLOCK
    k_pos = jnp.arange(S)
    qb = (q * (ATT_HEAD_DIM ** -0.5)).reshape(B, nb, Q_BLOCK, ATT_HEADS, 2, ATT_HEAD_DIM)
    qb = qb.transpose(1, 0, 2, 3, 4, 5)

    def block(args):
        q_blk, i = args
        q_pos = i * Q_BLOCK + jnp.arange(Q_BLOCK)
        allowed = (k_pos[None, :] // CHUNK) <= (q_pos[:, None] // CHUNK)
        dist = jnp.abs(q_pos[:, None] - k_pos[None, :]).astype(jnp.float32)
        bias = -slopes[:, None, None] * dist[None]
        s = jnp.einsum('bqhmd,bkhmd->bhmqk', q_blk, k).astype(jnp.float32)
        s = jnp.where(allowed, s + bias[None, :, None], -jnp.inf)
        p = jax.nn.softmax(s, axis=-1)
        a = p[:, :, 0] - lam * p[:, :, 1]
        return jnp.einsum('bhqk,bkhe->bqhe', a.astype(v.dtype), v)

    o = lax.map(block, (qb, jnp.arange(nb)))
    return o.transpose(1, 0, 2, 3, 4).reshape(B, S, ATT_HEADS, ATT_VDIM)


def spatial_gating(u, gv, w_s, b_s, g_v, g_out):
    B, S = u.shape[0], u.shape[1]
    nc = S // GMLP_BLOCK
    shp = (B, nc, GMLP_BLOCK, GMLP_GROUPS, GMLP_GROUP_DIM)
    gshp = (GMLP_GROUPS, GMLP_GROUP_DIM)
    gv = rmsnorm(gv.reshape(shp), g_v.reshape(gshp))
    mask = jnp.tril(jnp.ones((GMLP_BLOCK, GMLP_BLOCK), dtype=bool))
    w = jnp.where(mask[None], w_s, jnp.zeros_like(w_s))
    gate = jnp.einsum('gts,bnsgc->bntgc', w.astype(gv.dtype), gv) + b_s.T[None, None, :, :, None]
    y = rmsnorm(u.reshape(shp) * gate, g_out.reshape(gshp))
    return y.reshape(B, S, GMLP_WIDTH)


def peer(xn, w_q, sub_keys, w_u, w_v):
    B, S, D = xn.shape
    T = B * S
    xt = xn.reshape(T // PEER_TOKEN_BLOCK, PEER_TOKEN_BLOCK, D)
    K = PEER_TOPK

    def block(xb):
        q = (xb @ w_q).reshape(PEER_TOKEN_BLOCK, PEER_HEADS, 2, PEER_HALF)
        s = jnp.einsum('thpd,hpnd->thpn', q, sub_keys).astype(jnp.float32)
        vals, idx = lax.top_k(s, K)
        cand = vals[:, :, 0, :, None] + vals[:, :, 1, None, :]
        best, flat = lax.top_k(cand.reshape(PEER_TOKEN_BLOCK, PEER_HEADS, K * K), K)
        i1 = jnp.take_along_axis(idx[:, :, 0], flat // K, axis=-1)
        i2 = jnp.take_along_axis(idx[:, :, 1], flat % K, axis=-1)
        expert = i1 * PEER_KEYS + i2
        g = jax.nn.softmax(best, axis=-1)
        h = jnp.einsum('thkd,td->thk', w_u[expert], xb)
        a = (g * jax.nn.gelu(h.astype(jnp.float32))).astype(xb.dtype)
        return jnp.einsum('thk,thkd->td', a, w_v[expert])

    return lax.map(block, xt).reshape(B, S, D)


def setup_inputs(seed: int = 0) -> dict:
    key = jax.random.key(seed)
    ks = jax.random.split(key, 20)
    f32 = jnp.float32
    nrm = lambda k, shp, sc: jax.random.normal(k, shp, f32) * sc
    return {
        'x': nrm(ks[0], (BATCH, SEQ, D_MODEL), 1.0),
        'w_in': nrm(ks[1], (DEPTH, D_MODEL, IN_WIDTH), D_MODEL ** -0.5),
        'lam_q1': nrm(ks[2], (DEPTH, ATT_HEAD_DIM), 0.1),
        'lam_k1': nrm(ks[3], (DEPTH, ATT_HEAD_DIM), 0.1),
        'lam_q2': nrm(ks[4], (DEPTH, ATT_HEAD_DIM), 0.1),
        'lam_k2': nrm(ks[5], (DEPTH, ATT_HEAD_DIM), 0.1),
        'g_subln': 1.0 + nrm(ks[6], (DEPTH, ATT_VDIM), 0.02),
        'w_s': nrm(ks[7], (DEPTH, GMLP_GROUPS, GMLP_BLOCK, GMLP_BLOCK), GMLP_BLOCK ** -0.5),
        'b_s': 1.0 + nrm(ks[8], (DEPTH, GMLP_GROUPS, GMLP_BLOCK), 0.02),
        'g_gv': 1.0 + nrm(ks[9], (DEPTH, GMLP_WIDTH), 0.02),
        'g_gout': 1.0 + nrm(ks[10], (DEPTH, GMLP_WIDTH), 0.02),
        'w_out': nrm(ks[11], (DEPTH, MIX_WIDTH, D_MODEL), MIX_WIDTH ** -0.5),
        'g_mix': 1.0 + nrm(ks[12], (DEPTH, D_MODEL), 0.02),
        'g_ffn': 1.0 + nrm(ks[13], (DEPTH, D_MODEL), 0.02),
        'peer_wq': nrm(ks[14], (DEPTH, D_MODEL, PEER_HEADS * PEER_QDIM), D_MODEL ** -0.5),
        'peer_keys': nrm(ks[15], (DEPTH, PEER_HEADS, 2, PEER_KEYS, PEER_HALF), PEER_HALF ** -0.5),
        'peer_wu': nrm(ks[16], (DEPTH, PEER_EXPERTS, D_MODEL), D_MODEL ** -0.5),
        'peer_wv': nrm(ks[17], (DEPTH, PEER_EXPERTS, D_MODEL), PEER_HEADS ** -0.5),
        'g_final': 1.0 + nrm(ks[18], (D_MODEL,), 0.02),
    }


def reference(x, w_in, lam_q1, lam_k1, lam_q2, lam_k2, g_subln, w_s, b_s, g_gv, g_gout,
              w_out, g_mix, g_ffn, peer_wq, peer_keys, peer_wu, peer_wv, g_final):
    B, S, _ = x.shape
    slopes = alibi_slopes(ATT_HEADS)
    splits = [ATT_WIDTH, 2 * ATT_WIDTH, 3 * ATT_WIDTH, 3 * ATT_WIDTH + GMLP_WIDTH]
    for l in range(DEPTH):
        h = rmsnorm(x, g_mix[l])
        p = h @ w_in[l]
        q, k, v, u, gv = jnp.split(p, splits, axis=-1)
        q = q.reshape(B, S, ATT_HEADS, 2, ATT_HEAD_DIM)
        k = k.reshape(B, S, ATT_HEADS, 2, ATT_HEAD_DIM)
        v = v.reshape(B, S, ATT_HEADS, ATT_VDIM)
        lam_init = 0.8 - 0.6 * math.exp(-0.3 * l)
        lam = (jnp.exp(jnp.sum(lam_q1[l] * lam_k1[l]).astype(jnp.float32))
               - jnp.exp(jnp.sum(lam_q2[l] * lam_k2[l]).astype(jnp.float32)) + lam_init)
        o = diff_attention(q, k, v, lam, slopes)
        o = rmsnorm(o, g_subln[l]) * (1.0 - lam_init)
        y = spatial_gating(jax.nn.gelu(u), jax.nn.gelu(gv), w_s[l], b_s[l], g_gv[l], g_gout[l])
        mix = jnp.concatenate([o.reshape(B, S, ATT_WIDTH), y], axis=-1)
        x = x + mix @ w_out[l]
        x = x + peer(rmsnorm(x, g_ffn[l]), peer_wq[l], peer_keys[l], peer_wu[l], peer_wv[l])
    return rmsnorm(x, g_final)
```

```python
import functools
import math

import jax
import jax.numpy as jnp
from jax import lax
from jax.experimental import pallas as pl
from jax.experimental.pallas import tpu as pltpu

F32 = jnp.float32
BF16 = jnp.bfloat16
I32 = jnp.int32

EPS = 1e-6
LANES = 128
CHUNK = 64
ATT_HEADS = 4
ATT_HEAD_DIM = 64
ATT_VDIM = 2 * ATT_HEAD_DIM
ATT_WIDTH = ATT_HEADS * ATT_VDIM
GMLP_GROUPS = 4
GMLP_BLOCK = 128
GMLP_GROUP_DIM = 128
GMLP_WIDTH = GMLP_GROUPS * GMLP_GROUP_DIM
PEER_HEADS = 8
PEER_KEYS = 128
PEER_TOPK = 16
PEER_HALF = 128
LAM_INIT = 0.8 - 0.6 * math.exp(-0.3 * 0)

VMEM_LIMIT_BYTES = 48 * 1024 * 1024

_NT = (((1,), (1,)), ((), ()))


def _rms(x, g):
    return x * lax.rsqrt(jnp.mean(x * x, axis=-1, keepdims=True) + EPS) * g


def _inproj_kernel(x_ref, gmix_ref, win_ref, ws_ref, bsb_ref, ggv_ref, ggout_ref,
                   q_ref, k_ref, v_ref, y_ref):
    tm = x_ref.shape[0]
    h = _rms(x_ref[...], gmix_ref[...]).astype(BF16)

    def proj(lo, hi):
        return jnp.dot(h, win_ref[:, lo:hi], preferred_element_type=F32)

    aw = ATT_WIDTH
    q_ref[...] = (proj(0, aw) * (ATT_HEAD_DIM ** -0.5)).astype(BF16)
    k_ref[...] = proj(aw, 2 * aw).astype(BF16)
    v_ref[...] = proj(2 * aw, 3 * aw).astype(BF16)
    ug = jax.nn.gelu(proj(3 * aw, 3 * aw + GMLP_WIDTH))
    gg = jax.nn.gelu(proj(3 * aw + GMLP_WIDTH, 3 * aw + 2 * GMLP_WIDTH))

    row = lax.broadcasted_iota(I32, (GMLP_BLOCK, GMLP_BLOCK), 0)
    col = lax.broadcasted_iota(I32, (GMLP_BLOCK, GMLP_BLOCK), 1)
    for g in range(GMLP_GROUPS):
        cs = slice(g * GMLP_GROUP_DIM, (g + 1) * GMLP_GROUP_DIM)
        gvn = _rms(gg[:, cs], ggv_ref[:, cs]).astype(BF16)
        w = jnp.where(row >= col, ws_ref[g], 0.0).astype(BF16)
        for blk in range(tm // GMLP_BLOCK):
            rs = slice(blk * GMLP_BLOCK, (blk + 1) * GMLP_BLOCK)
            gate = jnp.dot(w, gvn[rs], preferred_element_type=F32) + bsb_ref[g]
            y_ref[rs, cs] = _rms(ug[rs, cs] * gate, ggout_ref[:, cs]).astype(BF16)


def _inproj_call(x2, g_mix, w_in, w_s, b_s, g_gv, g_gout, tm):
    t, d = x2.shape
    in_w = w_in.shape[1]
    bsb = jnp.broadcast_to(b_s[:, :, None], (GMLP_GROUPS, GMLP_BLOCK, GMLP_GROUP_DIM))
    const2 = lambda i: (0, 0)
    const3 = lambda i: (0, 0, 0)
    tile = lambda i: (i, 0)
    out_sds = jax.ShapeDtypeStruct((t, ATT_WIDTH), BF16)
    return pl.pallas_call(
        _inproj_kernel,
        grid=(t // tm,),
        in_specs=[
            pl.BlockSpec((tm, d), tile),
            pl.BlockSpec((1, d), const2),
            pl.BlockSpec((d, in_w), const2),
            pl.BlockSpec((GMLP_GROUPS, GMLP_BLOCK, GMLP_BLOCK), const3),
            pl.BlockSpec((GMLP_GROUPS, GMLP_BLOCK, GMLP_GROUP_DIM), const3),
            pl.BlockSpec((1, GMLP_WIDTH), const2),
            pl.BlockSpec((1, GMLP_WIDTH), const2),
        ],
        out_specs=[pl.BlockSpec((tm, ATT_WIDTH), tile)] * 4,
        out_shape=[out_sds] * 4,
        compiler_params=pltpu.CompilerParams(
            dimension_semantics=("arbitrary",), vmem_limit_bytes=VMEM_LIMIT_BYTES),
        name="inproj_gmlp",
    )(x2, g_mix.reshape(1, d), w_in.astype(BF16), w_s, bsb,
      g_gv.reshape(1, GMLP_WIDTH), g_gout.reshape(1, GMLP_WIDTH))


def _attn_kernel(slopes_ref, lq1_ref, lk1_ref, lq2_ref, lk2_ref, gsub_ref,
                 q_ref, k_ref, v_ref, o_ref, *, qb):
    seq = q_ref.shape[0]
    slope = slopes_ref[pl.program_id(1)]
    lam = (jnp.exp(jnp.sum(lq1_ref[...] * lk1_ref[...], axis=-1, keepdims=True))
           - jnp.exp(jnp.sum(lq2_ref[...] * lk2_ref[...], axis=-1, keepdims=True))
           + LAM_INIT)
    lane = lax.broadcasted_iota(I32, (qb, ATT_VDIM), 1)
    for qi in range(seq // qb):
        kv = (qi + 1) * qb
        qblk = q_ref[qi * qb:(qi + 1) * qb, :]
        kk = k_ref[0:kv, :]
        qpos = qi * qb + lax.broadcasted_iota(I32, (qb, kv), 0)
        kpos = lax.broadcasted_iota(I32, (qb, kv), 1)
        dist = jnp.abs(qpos - kpos).astype(F32)
        allowed = (kpos // CHUNK) <= (qpos // CHUNK)
        bm = jnp.where(allowed, -slope * dist, -jnp.inf)

        def softmax_parts(qm):
            s = lax.dot_general(qm, kk, _NT, preferred_element_type=F32) + bm
            p = jnp.exp(s - jnp.max(s, axis=-1, keepdims=True))
            return p, jnp.sum(p, axis=-1, keepdims=True)

        p0, l0 = softmax_parts(jnp.where(lane < ATT_HEAD_DIM, qblk, jnp.zeros_like(qblk)))
        p1, l1 = softmax_parts(jnp.where(lane >= ATT_HEAD_DIM, qblk, jnp.zeros_like(qblk)))
        a = p0 * (1.0 / l0) - p1 * (lam / l1)
        o = jnp.dot(a.astype(BF16), v_ref[0:kv, :], preferred_element_type=F32)
        o = _rms(o, gsub_ref[...]) * (1.0 - LAM_INIT)
        o_ref[qi * qb:(qi + 1) * qb, :] = o.astype(BF16)


def _attn_call(q, k, v, lam_q1, lam_k1, lam_q2, lam_k2, g_subln, batch, seq, qb):
    t = q.shape[0]
    slopes = 2.0 ** (-8.0 * jnp.arange(1, ATT_HEADS + 1, dtype=F32) / ATT_HEADS)
    blk = pl.BlockSpec((seq, ATT_VDIM), lambda b, h: (b, h))
    vec = lambda n: pl.BlockSpec((1, n), lambda b, h: (0, 0))
    return pl.pallas_call(
        functools.partial(_attn_kernel, qb=qb),
        grid=(batch, ATT_HEADS),
        in_specs=[pl.BlockSpec(memory_space=pltpu.SMEM)]
                 + [vec(ATT_HEAD_DIM)] * 4 + [vec(ATT_VDIM)] + [blk] * 3,
        out_specs=blk,
        out_shape=jax.ShapeDtypeStruct((t, ATT_WIDTH), BF16),
        compiler_params=pltpu.CompilerParams(
            dimension_semantics=("arbitrary", "arbitrary"),
            vmem_limit_bytes=VMEM_LIMIT_BYTES),
        name="diff_attn",
    )(slopes, lam_q1.reshape(1, -1), lam_k1.reshape(1, -1), lam_q2.reshape(1, -1),
      lam_k2.reshape(1, -1), g_subln.reshape(1, -1), q, k, v)


def _outproj_kernel(o_ref, y_ref, x_ref, wo_ref, gffn_ref, wq_ref, keys_ref,
                    x1_ref, xn_ref, st_ref):
    x1 = (x_ref[...]
          + jnp.dot(o_ref[...], wo_ref[0:ATT_WIDTH, :], preferred_element_type=F32)
          + jnp.dot(y_ref[...], wo_ref[ATT_WIDTH:, :], preferred_element_type=F32))
    x1_ref[...] = x1
    xn = _rms(x1, gffn_ref[...]).astype(BF16)
    xn_ref[...] = xn
    qp = jnp.dot(xn, wq_ref[...], preferred_element_type=F32).astype(BF16)
    for hp in range(2 * PEER_HEADS):
        st_ref[hp] = lax.dot_general(
            keys_ref[hp], qp[:, hp * PEER_HALF:(hp + 1) * PEER_HALF], _NT,
            preferred_element_type=F32)


def _outproj_call(o, y, x2, w_out, g_ffn, peer_wq, peer_keys, tm):
    t, d = x2.shape
    nq = peer_wq.shape[1]
    tile = lambda i: (i, 0)
    const2 = lambda i: (0, 0)
    keys = peer_keys.reshape(2 * PEER_HEADS, PEER_KEYS, PEER_HALF).astype(BF16)
    return pl.pallas_call(
        _outproj_kernel,
        grid=(t // tm,),
        in_specs=[
            pl.BlockSpec((tm, ATT_WIDTH), tile),
            pl.BlockSpec((tm, GMLP_WIDTH), tile),
            pl.BlockSpec((tm, d), tile),
            pl.BlockSpec((ATT_WIDTH + GMLP_WIDTH, d), const2),
            pl.BlockSpec((1, d), const2),
            pl.BlockSpec((d, nq), const2),
            pl.BlockSpec((2 * PEER_HEADS, PEER_KEYS, PEER_HALF), lambda i: (0, 0, 0)),
        ],
        out_specs=[
            pl.BlockSpec((tm, d), tile),
            pl.BlockSpec((tm, d), tile),
            pl.BlockSpec((2 * PEER_HEADS, PEER_KEYS, tm), lambda i: (0, 0, i)),
        ],
        out_shape=[
            jax.ShapeDtypeStruct((t, d), F32),
            jax.ShapeDtypeStruct((t, d), BF16),
            jax.ShapeDtypeStruct((2 * PEER_HEADS, PEER_KEYS, t), F32),
        ],
        compiler_params=pltpu.CompilerParams(
            dimension_semantics=("arbitrary",), vmem_limit_bytes=VMEM_LIMIT_BYTES),
        name="outproj_scores",
    )(o, y, x2, w_out.astype(BF16), g_ffn.reshape(1, d), peer_wq.astype(BF16), keys)


def _extract_topk(s, ids, k):
    big = jnp.iinfo(jnp.int32).max
    vals, picks = [], []
    for _ in range(k):
        m = jnp.max(s, axis=0, keepdims=True)
        pick = jnp.min(jnp.where(s == m, ids, big), axis=0, keepdims=True)
        vals.append(m)
        picks.append(pick)
        s = jnp.where(ids == pick, -jnp.inf, s)
    return jnp.concatenate(vals, axis=0), jnp.concatenate(picks, axis=0)


def _take_rows(table, sel, n):
    out = jnp.zeros(sel.shape, table.dtype)
    for j in range(n):
        out = jnp.where(sel == j, table[j:j + 1, :], out)
    return out


def _route_kernel(st_ref, a_ref, b_ref, g_ref):
    tt = st_ref.shape[2]
    k = PEER_TOPK
    key_ids = lax.broadcasted_iota(I32, (PEER_KEYS, tt), 0)
    widths = [k] + [k // 2] * (k - 1)
    flat = jnp.concatenate(
        [k1 * k + lax.broadcasted_iota(I32, (w, tt), 0) for k1, w in enumerate(widths)], axis=0)

    def head(h, carry):
        v1, i1 = _extract_topk(st_ref[2 * h], key_ids, k)
        v2, i2 = _extract_topk(st_ref[2 * h + 1], key_ids, k)
        cand = jnp.concatenate(
            [v1[k1:k1 + 1, :] + v2[0:w, :] for k1, w in enumerate(widths)], axis=0)
        best, pick = _extract_topk(cand, flat, k)
        sel1 = pick // k
        sel2 = pick - sel1 * k
        e = jnp.exp(best - best[0:1, :])
        rows = pl.ds(pl.multiple_of(h * k, k), k)
        a_ref[rows, :] = _take_rows(i1, sel1, k)
        b_ref[rows, :] = _take_rows(i2, sel2, k)
        g_ref[rows, :] = e / jnp.sum(e, axis=0, keepdims=True)
        return carry

    lax.fori_loop(0, PEER_HEADS, head, 0)


def _route_call(st, tt):
    t = st.shape[2]
    hk = PEER_HEADS * PEER_TOPK
    out_spec = pl.BlockSpec((hk, tt), lambda i: (0, i))
    return pl.pallas_call(
        _route_kernel,
        grid=(t // tt,),
        in_specs=[pl.BlockSpec((2 * PEER_HEADS, PEER_KEYS, tt), lambda i: (0, 0, i))],
        out_specs=[out_spec] * 3,
        out_shape=[jax.ShapeDtypeStruct((hk, t), I32),
                   jax.ShapeDtypeStruct((hk, t), I32),
                   jax.ShapeDtypeStruct((hk, t), F32)],
        compiler_params=pltpu.CompilerParams(
            dimension_semantics=("arbitrary",), vmem_limit_bytes=VMEM_LIMIT_BYTES),
        name="peer_route",
    )(st)


W_PITCH = 136


def _peer_kernel(xn_ref, x1_ref, at_ref, bt_ref, gt_ref, wut_ref, wv_ref, gfin_ref,
                 out_ref, wbuf, acc, a_s, b_s, g_s, *, cw):
    tb = xn_ref.shape[0]
    c = pl.program_id(1)

    @pl.when(c == 0)
    def _build_gates():
        a_s[...] = at_ref[...].T
        b_s[...] = bt_ref[...].T
        g_s[...] = gt_ref[...].T
        ids = lax.broadcasted_iota(I32, (PEER_KEYS, PEER_HEADS * PEER_TOPK), 0)

        def token(t, carry):
            arow = a_s[pl.ds(t, 1), :]
            brow = b_s[pl.ds(t, 1), :]
            grow = g_s[pl.ds(t, 1), :]
            ga = jnp.where(ids == arow, grow, 0.0).astype(BF16)
            ob = jnp.where(ids == brow, 1.0, 0.0).astype(BF16)
            wt = lax.dot_general(ga, ob, _NT, preferred_element_type=F32)
            wbuf[pl.ds(pl.multiple_of(t * W_PITCH, 8), PEER_KEYS), :] = wt
            return carry

        lax.fori_loop(0, tb, token, 0)
        acc[...] = jnp.zeros_like(acc)

    hid = jnp.dot(xn_ref[...], wut_ref[...], preferred_element_type=F32)
    gate = jnp.concatenate(
        [wbuf[pl.ds(c * cw + j, tb, stride=W_PITCH), :] for j in range(cw)], axis=1)
    z = (jax.nn.gelu(hid) * gate).astype(BF16)
    acc[...] += jnp.dot(z, wv_ref[...], preferred_element_type=F32)

    @pl.when(c == pl.num_programs(1) - 1)
    def _finish():
        out_ref[...] = _rms(x1_ref[...] + acc[...], gfin_ref[...])


def _peer_call(xn, x1, a_t, b_t, g_t, peer_wu, peer_wv, g_final, tb, cw):
    t, d = xn.shape
    n_exp = peer_wu.shape[0]
    hk = PEER_HEADS * PEER_TOPK
    wut = peer_wu.astype(BF16).T
    wv = peer_wv.astype(BF16)
    tok = lambda i, c: (i, 0)
    tab = pl.BlockSpec((hk, tb), lambda i, c: (0, i))
    return pl.pallas_call(
        functools.partial(_peer_kernel, cw=cw),
        grid=(t // tb, PEER_KEYS // cw),
        in_specs=[
            pl.BlockSpec((tb, d), tok),
            pl.BlockSpec((tb, d), tok),
            tab, tab, tab,
            pl.BlockSpec((d, cw * PEER_KEYS), lambda i, c: (0, c)),
            pl.BlockSpec((cw * PEER_KEYS, d), lambda i, c: (c, 0)),
            pl.BlockSpec((1, d), lambda i, c: (0, 0)),
        ],
        out_specs=pl.BlockSpec((tb, d), tok),
        out_shape=jax.ShapeDtypeStruct((t, d), F32),
        scratch_shapes=[
            pltpu.VMEM((tb * W_PITCH, PEER_KEYS), F32),
            pltpu.VMEM((tb, d), F32),
            pltpu.VMEM((tb, hk), I32),
            pltpu.VMEM((tb, hk), I32),
            pltpu.VMEM((tb, hk), F32),
        ],
        compiler_params=pltpu.CompilerParams(
            dimension_semantics=("arbitrary", "arbitrary"),
            vmem_limit_bytes=VMEM_LIMIT_BYTES),
        name="peer_dense",
    )(xn, x1, a_t, b_t, g_t, wut, wv, g_final.reshape(1, d))


def kernel(x, w_in, lam_q1, lam_k1, lam_q2, lam_k2, g_subln, w_s, b_s, g_gv, g_gout,
           w_out, g_mix, g_ffn, peer_wq, peer_keys, peer_wu, peer_wv, g_final):
    batch, seq, d = x.shape
    depth = w_in.shape[0]
    assert depth == 1, "LAM_INIT and the single pass below assume one layer"
    t = batch * seq
    x2 = x.reshape(t, d)
    l = 0
    q, k, v, y = _inproj_call(x2, g_mix[l], w_in[l], w_s[l], b_s[l], g_gv[l], g_gout[l],
                              tm=min(512, t))
    o = _attn_call(q, k, v, lam_q1[l], lam_k1[l], lam_q2[l], lam_k2[l], g_subln[l],
                   batch, seq, qb=min(256, seq))
    x1, xn, st = _outproj_call(o, y, x2, w_out[l], g_ffn[l], peer_wq[l], peer_keys[l],
                               tm=min(512, t))
    a_t, b_t, g_t = _route_call(st, tt=min(256, t))
    out = _peer_call(xn, x1, a_t, b_t, g_t, peer_wu[l], peer_wv[l], g_final,
                     tb=min(256, t), cw=8)
    return out.reshape(batch, seq, d)
```

```python
import functools
import math

import jax
import jax.numpy as jnp
from jax import lax
from jax.experimental import pallas as pl
from jax.experimental.pallas import tpu as pltpu

F32 = jnp.float32
BF16 = jnp.bfloat16
I32 = jnp.int32

EPS = 1e-6
LANES = 128
CHUNK = 64
ATT_HEADS = 4
ATT_HEAD_DIM = 64
ATT_VDIM = 2 * ATT_HEAD_DIM
ATT_WIDTH = ATT_HEADS * ATT_VDIM
GMLP_GROUPS = 4
GMLP_BLOCK = 128
GMLP_GROUP_DIM = 128
GMLP_WIDTH = GMLP_GROUPS * GMLP_GROUP_DIM
PEER_HEADS = 8
PEER_KEYS = 128
PEER_TOPK = 16
PEER_HALF = 128
LAM_INIT = 0.8 - 0.6 * math.exp(-0.3 * 0)

VMEM_LIMIT_BYTES = 48 * 1024 * 1024
PEER_VMEM_LIMIT_BYTES = 56 * 1024 * 1024

_NT = (((1,), (1,)), ((), ()))


def _rms(x, g):
    return x * lax.rsqrt(jnp.mean(x * x, axis=-1, keepdims=True) + EPS) * g


def _inproj_kernel(x_ref, gmix_ref, win_ref, ws_ref, bsb_ref, ggv_ref, ggout_ref,
                   q_ref, k_ref, v_ref, y_ref):
    tm = x_ref.shape[0]
    h = _rms(x_ref[...], gmix_ref[...]).astype(BF16)

    def proj(lo, hi):
        return jnp.dot(h, win_ref[:, lo:hi], preferred_element_type=F32)

    aw = ATT_WIDTH
    q_ref[...] = (proj(0, aw) * (ATT_HEAD_DIM ** -0.5)).astype(BF16)
    k_ref[...] = proj(aw, 2 * aw).astype(BF16)
    v_ref[...] = proj(2 * aw, 3 * aw).astype(BF16)
    ug = jax.nn.gelu(proj(3 * aw, 3 * aw + GMLP_WIDTH))
    gg = jax.nn.gelu(proj(3 * aw + GMLP_WIDTH, 3 * aw + 2 * GMLP_WIDTH))

    row = lax.broadcasted_iota(I32, (GMLP_BLOCK, GMLP_BLOCK), 0)
    col = lax.broadcasted_iota(I32, (GMLP_BLOCK, GMLP_BLOCK), 1)
    for g in range(GMLP_GROUPS):
        cs = slice(g * GMLP_GROUP_DIM, (g + 1) * GMLP_GROUP_DIM)
        gvn = _rms(gg[:, cs], ggv_ref[:, cs]).astype(BF16)
        w = jnp.where(row >= col, ws_ref[g], 0.0).astype(BF16)
        for blk in range(tm // GMLP_BLOCK):
            rs = slice(blk * GMLP_BLOCK, (blk + 1) * GMLP_BLOCK)
            gate = jnp.dot(w, gvn[rs], preferred_element_type=F32) + bsb_ref[g]
            y_ref[rs, cs] = _rms(ug[rs, cs] * gate, ggout_ref[:, cs]).astype(BF16)


def _inproj_call(x2, g_mix, w_in, w_s, b_s, g_gv, g_gout, tm):
    t, d = x2.shape
    in_w = w_in.shape[1]
    bsb = jnp.broadcast_to(b_s[:, :, None], (GMLP_GROUPS, GMLP_BLOCK, GMLP_GROUP_DIM))
    const2 = lambda i: (0, 0)
    const3 = lambda i: (0, 0, 0)
    tile = lambda i: (i, 0)
    out_sds = jax.ShapeDtypeStruct((t, ATT_WIDTH), BF16)
    return pl.pallas_call(
        _inproj_kernel,
        grid=(t // tm,),
        in_specs=[
            pl.BlockSpec((tm, d), tile),
            pl.BlockSpec((1, d), const2),
            pl.BlockSpec((d, in_w), const2),
            pl.BlockSpec((GMLP_GROUPS, GMLP_BLOCK, GMLP_BLOCK), const3),
            pl.BlockSpec((GMLP_GROUPS, GMLP_BLOCK, GMLP_GROUP_DIM), const3),
            pl.BlockSpec((1, GMLP_WIDTH), const2),
            pl.BlockSpec((1, GMLP_WIDTH), const2),
        ],
        out_specs=[pl.BlockSpec((tm, ATT_WIDTH), tile)] * 4,
        out_shape=[out_sds] * 4,
        compiler_params=pltpu.CompilerParams(
            dimension_semantics=("arbitrary",), vmem_limit_bytes=VMEM_LIMIT_BYTES),
        name="inproj_gmlp",
    )(x2, g_mix.reshape(1, d), w_in.astype(BF16), w_s, bsb,
      g_gv.reshape(1, GMLP_WIDTH), g_gout.reshape(1, GMLP_WIDTH))


def _attn_kernel(slopes_ref, lq1_ref, lk1_ref, lq2_ref, lk2_ref, gsub_ref,
                 q_ref, k_ref, v_ref, o_ref, *, qb):
    seq = q_ref.shape[0]
    slope = slopes_ref[pl.program_id(1)]
    lam = (jnp.exp(jnp.sum(lq1_ref[...] * lk1_ref[...], axis=-1, keepdims=True))
           - jnp.exp(jnp.sum(lq2_ref[...] * lk2_ref[...], axis=-1, keepdims=True))
           + LAM_INIT)
    lane = lax.broadcasted_iota(I32, (qb, ATT_VDIM), 1)
    for qi in range(seq // qb):
        kv = (qi + 1) * qb
        qblk = q_ref[qi * qb:(qi + 1) * qb, :]
        kk = k_ref[0:kv, :]
        qpos = qi * qb + lax.broadcasted_iota(I32, (qb, kv), 0)
        kpos = lax.broadcasted_iota(I32, (qb, kv), 1)
        dist = jnp.abs(qpos - kpos).astype(F32)
        allowed = (kpos // CHUNK) <= (qpos // CHUNK)
        bm = jnp.where(allowed, -slope * dist, -jnp.inf)

        def softmax_parts(qm):
            s = lax.dot_general(qm, kk, _NT, preferred_element_type=F32) + bm
            p = jnp.exp(s - jnp.max(s, axis=-1, keepdims=True))
            return p, jnp.sum(p, axis=-1, keepdims=True)

        p0, l0 = softmax_parts(jnp.where(lane < ATT_HEAD_DIM, qblk, jnp.zeros_like(qblk)))
        p1, l1 = softmax_parts(jnp.where(lane >= ATT_HEAD_DIM, qblk, jnp.zeros_like(qblk)))
        a = p0 * (1.0 / l0) - p1 * (lam / l1)
        o = jnp.dot(a.astype(BF16), v_ref[0:kv, :], preferred_element_type=F32)
        o = _rms(o, gsub_ref[...]) * (1.0 - LAM_INIT)
        o_ref[qi * qb:(qi + 1) * qb, :] = o.astype(BF16)


def _attn_call(q, k, v, lam_q1, lam_k1, lam_q2, lam_k2, g_subln, batch, seq, qb):
    t = q.shape[0]
    slopes = 2.0 ** (-8.0 * jnp.arange(1, ATT_HEADS + 1, dtype=F32) / ATT_HEADS)
    blk = pl.BlockSpec((seq, ATT_VDIM), lambda b, h: (b, h))
    vec = lambda n: pl.BlockSpec((1, n), lambda b, h: (0, 0))
    return pl.pallas_call(
        functools.partial(_attn_kernel, qb=qb),
        grid=(batch, ATT_HEADS),
        in_specs=[pl.BlockSpec(memory_space=pltpu.SMEM)]
                 + [vec(ATT_HEAD_DIM)] * 4 + [vec(ATT_VDIM)] + [blk] * 3,
        out_specs=blk,
        out_shape=jax.ShapeDtypeStruct((t, ATT_WIDTH), BF16),
        compiler_params=pltpu.CompilerParams(
            dimension_semantics=("arbitrary", "arbitrary"),
            vmem_limit_bytes=VMEM_LIMIT_BYTES),
        name="diff_attn",
    )(slopes, lam_q1.reshape(1, -1), lam_k1.reshape(1, -1), lam_q2.reshape(1, -1),
      lam_k2.reshape(1, -1), g_subln.reshape(1, -1), q, k, v)


def _outproj_kernel(o_ref, y_ref, x_ref, wo_ref, gffn_ref, wq_ref, keys_ref,
                    x1_ref, xn_ref, st_ref):
    x1 = (x_ref[...]
          + jnp.dot(o_ref[...], wo_ref[0:ATT_WIDTH, :], preferred_element_type=F32)
          + jnp.dot(y_ref[...], wo_ref[ATT_WIDTH:, :], preferred_element_type=F32))
    x1_ref[...] = x1
    xn = _rms(x1, gffn_ref[...]).astype(BF16)
    xn_ref[...] = xn
    qp = jnp.dot(xn, wq_ref[...], preferred_element_type=F32).astype(BF16)
    for hp in range(2 * PEER_HEADS):
        st_ref[hp] = lax.dot_general(
            keys_ref[hp], qp[:, hp * PEER_HALF:(hp + 1) * PEER_HALF], _NT,
            preferred_element_type=F32)


def _outproj_call(o, y, x2, w_out, g_ffn, peer_wq, peer_keys, tm):
    t, d = x2.shape
    nq = peer_wq.shape[1]
    tile = lambda i: (i, 0)
    const2 = lambda i: (0, 0)
    keys = peer_keys.reshape(2 * PEER_HEADS, PEER_KEYS, PEER_HALF).astype(BF16)
    return pl.pallas_call(
        _outproj_kernel,
        grid=(t // tm,),
        in_specs=[
            pl.BlockSpec((tm, ATT_WIDTH), tile),
            pl.BlockSpec((tm, GMLP_WIDTH), tile),
            pl.BlockSpec((tm, d), tile),
            pl.BlockSpec((ATT_WIDTH + GMLP_WIDTH, d), const2),
            pl.BlockSpec((1, d), const2),
            pl.BlockSpec((d, nq), const2),
            pl.BlockSpec((2 * PEER_HEADS, PEER_KEYS, PEER_HALF), lambda i: (0, 0, 0)),
        ],
        out_specs=[
            pl.BlockSpec((tm, d), tile),
            pl.BlockSpec((tm, d), tile),
            pl.BlockSpec((2 * PEER_HEADS, PEER_KEYS, tm), lambda i: (0, 0, i)),
        ],
        out_shape=[
            jax.ShapeDtypeStruct((t, d), F32),
            jax.ShapeDtypeStruct((t, d), BF16),
            jax.ShapeDtypeStruct((2 * PEER_HEADS, PEER_KEYS, t), F32),
        ],
        compiler_params=pltpu.CompilerParams(
            dimension_semantics=("arbitrary",), vmem_limit_bytes=VMEM_LIMIT_BYTES),
        name="outproj_scores",
    )(o, y, x2, w_out.astype(BF16), g_ffn.reshape(1, d), peer_wq.astype(BF16), keys)


def _extract_topk(s, ids, k):
    big = jnp.iinfo(jnp.int32).max
    vals, picks = [], []
    for _ in range(k):
        m = jnp.max(s, axis=0, keepdims=True)
        pick = jnp.min(jnp.where(s == m, ids, big), axis=0, keepdims=True)
        vals.append(m)
        picks.append(pick)
        s = jnp.where(ids == pick, -jnp.inf, s)
    return jnp.concatenate(vals, axis=0), jnp.concatenate(picks, axis=0)


def _take_rows(table, sel, n):
    out = jnp.zeros(sel.shape, table.dtype)
    for j in range(n):
        out = jnp.where(sel == j, table[j:j + 1, :], out)
    return out


def _route_kernel(st_ref, a_ref, b_ref, g_ref):
    tt = st_ref.shape[2]
    k = PEER_TOPK
    key_ids = lax.broadcasted_iota(I32, (PEER_KEYS, tt), 0)
    widths = [k] + [k // 2] * (k - 1)
    flat = jnp.concatenate(
        [k1 * k + lax.broadcasted_iota(I32, (w, tt), 0) for k1, w in enumerate(widths)], axis=0)

    def head(h, carry):
        v1, i1 = _extract_topk(st_ref[2 * h], key_ids, k)
        v2, i2 = _extract_topk(st_ref[2 * h + 1], key_ids, k)
        cand = jnp.concatenate(
            [v1[k1:k1 + 1, :] + v2[0:w, :] for k1, w in enumerate(widths)], axis=0)
        best, pick = _extract_topk(cand, flat, k)
        sel1 = pick // k
        sel2 = pick - sel1 * k
        e = jnp.exp(best - best[0:1, :])
        rows = pl.ds(pl.multiple_of(h * k, k), k)
        a_ref[rows, :] = _take_rows(i1, sel1, k)
        b_ref[rows, :] = _take_rows(i2, sel2, k)
        g_ref[rows, :] = e / jnp.sum(e, axis=0, keepdims=True)
        return carry

    lax.fori_loop(0, PEER_HEADS, head, 0)


def _route_call(st, tt):
    t = st.shape[2]
    hk = PEER_HEADS * PEER_TOPK
    out_spec = pl.BlockSpec((hk, tt), lambda i: (0, i))
    return pl.pallas_call(
        _route_kernel,
        grid=(t // tt,),
        in_specs=[pl.BlockSpec((2 * PEER_HEADS, PEER_KEYS, tt), lambda i: (0, 0, i))],
        out_specs=[out_spec] * 3,
        out_shape=[jax.ShapeDtypeStruct((hk, t), I32),
                   jax.ShapeDtypeStruct((hk, t), I32),
                   jax.ShapeDtypeStruct((hk, t), F32)],
        compiler_params=pltpu.CompilerParams(
            dimension_semantics=("arbitrary",), vmem_limit_bytes=VMEM_LIMIT_BYTES),
        name="peer_route",
    )(st)


W_PAIRS = PEER_KEYS // 2
W_PITCH = 72
BUILD_UNROLL = 32
HI16 = 0xFFFF0000


def _peer_kernel(xn_ref, x1_ref, at_ref, bt_ref, gt_ref, wut_ref, wv_ref, gfin_ref,
                 out_ref, wbuf, acc, zs, a_s, b_s, g_s, *, cw, rb):
    tb = xn_ref.shape[0]
    c = pl.program_id(1)
    n_chunks = pl.num_programs(1) - 1
    slots = PEER_HEADS * PEER_TOPK

    @pl.when(c == 0)
    def _build_gates():
        acc[...] = jnp.zeros_like(acc)
        zs[1] = jnp.zeros(zs.shape[1:], zs.dtype)
        a_s[...] = at_ref[...].T
        b_s[...] = bt_ref[...].T
        g_s[...] = gt_ref[...].T
        r = lax.broadcasted_iota(I32, (PEER_KEYS, slots), 0)
        ids_a = jnp.where(r < W_PAIRS, 2 * r, 2 * r - (PEER_KEYS - 1))
        ids_b = r

        def tokens(i, carry):
            for u in range(BUILD_UNROLL):
                t = i * BUILD_UNROLL + u
                arow = a_s[pl.ds(t, 1), :]
                brow = b_s[pl.ds(t, 1), :]
                grow = g_s[pl.ds(t, 1), :]
                ga = jnp.where(ids_a == arow, grow, 0.0).astype(BF16)
                ob = jnp.where(ids_b == brow, 1.0, 0.0).astype(BF16)
                wt = lax.dot_general(ga, ob, _NT, preferred_element_type=F32)
                bits = lax.bitcast_convert_type(wt, jnp.uint32)
                packed = (bits[W_PAIRS:] & jnp.uint32(HI16)) | (bits[:W_PAIRS] >> 16)
                wbuf[pl.ds(pl.multiple_of(t * W_PITCH, 8), W_PAIRS), :] = packed
            return carry

        lax.fori_loop(0, tb // BUILD_UNROLL, tokens, 0)

    wr = lax.rem(c, 2)
    cc = jnp.minimum(c, n_chunks - 1)
    xn = xn_ref[...]
    for jp in range(cw // 2):
        cols = slice(jp * 2 * PEER_KEYS, (jp + 1) * 2 * PEER_KEYS)
        hid = jnp.dot(xn, wut_ref[:, cols], preferred_element_type=F32)
        w = wbuf[pl.ds(cc * (cw // 2) + jp, tb, stride=W_PITCH), :]
        gate = jnp.concatenate(
            [lax.bitcast_convert_type(w << 16, F32),
             lax.bitcast_convert_type(w & jnp.uint32(HI16), F32)], axis=1)
        zs[wr, :, cols] = (jax.nn.gelu(hid) * gate).astype(BF16)
    acc[...] += jnp.dot(zs[1 - wr], wv_ref[...], preferred_element_type=F32)

    @pl.when(c == n_chunks)
    def _finish():
        out_ref[...] = _rms(x1_ref[...] + acc[...], gfin_ref[...])


def _peer_call(xn, x1, a_t, b_t, g_t, peer_wu, peer_wv, g_final, tb, cw, rb):
    t, d = xn.shape
    n_exp = peer_wu.shape[0]
    hk = PEER_HEADS * PEER_TOPK
    wut = peer_wu.astype(BF16).T
    wv = peer_wv.astype(BF16)
    n_chunks = PEER_KEYS // cw
    tok = lambda i, c: (i, 0)
    tab = pl.BlockSpec((hk, tb), lambda i, c: (0, i))
    return pl.pallas_call(
        functools.partial(_peer_kernel, cw=cw, rb=rb),
        grid=(t // tb, n_chunks + 1),
        in_specs=[
            pl.BlockSpec((tb, d), tok),
            pl.BlockSpec((tb, d), tok),
            tab, tab, tab,
            pl.BlockSpec((d, cw * PEER_KEYS), lambda i, c: (0, jnp.minimum(c, n_chunks - 1))),
            pl.BlockSpec((cw * PEER_KEYS, d), lambda i, c: (jnp.maximum(c - 1, 0), 0)),
            pl.BlockSpec((1, d), lambda i, c: (0, 0)),
        ],
        out_specs=pl.BlockSpec((tb, d), tok),
        out_shape=jax.ShapeDtypeStruct((t, d), F32),
        scratch_shapes=[
            pltpu.VMEM((tb * W_PITCH, PEER_KEYS), jnp.uint32),
            pltpu.VMEM((tb, d), F32),
            pltpu.VMEM((2, tb, cw * PEER_KEYS), BF16),
            pltpu.VMEM((tb, hk), I32),
            pltpu.VMEM((tb, hk), I32),
            pltpu.VMEM((tb, hk), F32),
        ],
        compiler_params=pltpu.CompilerParams(
            dimension_semantics=("arbitrary", "arbitrary"),
            vmem_limit_bytes=PEER_VMEM_LIMIT_BYTES),
        name="peer_dense",
    )(xn, x1, a_t, b_t, g_t, wut, wv, g_final.reshape(1, d))


def kernel(x, w_in, lam_q1, lam_k1, lam_q2, lam_k2, g_subln, w_s, b_s, g_gv, g_gout,
           w_out, g_mix, g_ffn, peer_wq, peer_keys, peer_wu, peer_wv, g_final):
    batch, seq, d = x.shape
    depth = w_in.shape[0]
    assert depth == 1, "LAM_INIT and the single pass below assume one layer"
    t = batch * seq
    x2 = x.reshape(t, d)
    l = 0
    q, k, v, y = _inproj_call(x2, g_mix[l], w_in[l], w_s[l], b_s[l], g_gv[l], g_gout[l],
                              tm=min(512, t))
    o = _attn_call(q, k, v, lam_q1[l], lam_k1[l], lam_q2[l], lam_k2[l], g_subln[l],
                   batch, seq, qb=min(256, seq))
    x1, xn, st = _outproj_call(o, y, x2, w_out[l], g_ffn[l], peer_wq[l], peer_keys[l],
                               tm=min(512, t))
    a_t, b_t, g_t = _route_call(st, tt=min(256, t))
    out = _peer_call(xn, x1, a_t, b_t, g_t, peer_wu[l], peer_wv[l], g_final,
                     tb=min(512, t), cw=8, rb=256)
    return out.reshape(batch, seq, d)
```

```python
import functools
import math

import jax
import jax.numpy as jnp
from jax import lax
from jax.experimental import pallas as pl
from jax.experimental.pallas import tpu as pltpu

F32 = jnp.float32
BF16 = jnp.bfloat16
I32 = jnp.int32

EPS = 1e-6
LANES = 128
CHUNK = 64
ATT_HEADS = 4
ATT_HEAD_DIM = 64
ATT_VDIM = 2 * ATT_HEAD_DIM
ATT_WIDTH = ATT_HEADS * ATT_VDIM
GMLP_GROUPS = 4
GMLP_BLOCK = 128
GMLP_GROUP_DIM = 128
GMLP_WIDTH = GMLP_GROUPS * GMLP_GROUP_DIM
PEER_HEADS = 8
PEER_KEYS = 128
PEER_TOPK = 16
PEER_HALF = 128
LAM_INIT = 0.8 - 0.6 * math.exp(-0.3 * 0)

VMEM_LIMIT_BYTES = 48 * 1024 * 1024
PEER_VMEM_LIMIT_BYTES = 56 * 1024 * 1024

_NT = (((1,), (1,)), ((), ()))


def _rms(x, g):
    return x * lax.rsqrt(jnp.mean(x * x, axis=-1, keepdims=True) + EPS) * g


def _inproj_kernel(x_ref, gmix_ref, win_ref, ws_ref, bsb_ref, ggv_ref, ggout_ref,
                   q_ref, k_ref, v_ref, y_ref):
    tm = x_ref.shape[0]
    h = _rms(x_ref[...], gmix_ref[...]).astype(BF16)

    def proj(lo, hi):
        return jnp.dot(h, win_ref[:, lo:hi], preferred_element_type=F32)

    aw = ATT_WIDTH
    q_ref[...] = (proj(0, aw) * (ATT_HEAD_DIM ** -0.5)).astype(BF16)
    k_ref[...] = proj(aw, 2 * aw).astype(BF16)
    v_ref[...] = proj(2 * aw, 3 * aw).astype(BF16)
    ug = jax.nn.gelu(proj(3 * aw, 3 * aw + GMLP_WIDTH))
    gg = jax.nn.gelu(proj(3 * aw + GMLP_WIDTH, 3 * aw + 2 * GMLP_WIDTH))

    row = lax.broadcasted_iota(I32, (GMLP_BLOCK, GMLP_BLOCK), 0)
    col = lax.broadcasted_iota(I32, (GMLP_BLOCK, GMLP_BLOCK), 1)
    for g in range(GMLP_GROUPS):
        cs = slice(g * GMLP_GROUP_DIM, (g + 1) * GMLP_GROUP_DIM)
        gvn = _rms(gg[:, cs], ggv_ref[:, cs]).astype(BF16)
        w = jnp.where(row >= col, ws_ref[g], 0.0).astype(BF16)
        for blk in range(tm // GMLP_BLOCK):
            rs = slice(blk * GMLP_BLOCK, (blk + 1) * GMLP_BLOCK)
            gate = jnp.dot(w, gvn[rs], preferred_element_type=F32) + bsb_ref[g]
            y_ref[rs, cs] = _rms(ug[rs, cs] * gate, ggout_ref[:, cs]).astype(BF16)


def _inproj_call(x2, g_mix, w_in, w_s, b_s, g_gv, g_gout, tm):
    t, d = x2.shape
    in_w = w_in.shape[1]
    bsb = jnp.broadcast_to(b_s[:, :, None], (GMLP_GROUPS, GMLP_BLOCK, GMLP_GROUP_DIM))
    const2 = lambda i: (0, 0)
    const3 = lambda i: (0, 0, 0)
    tile = lambda i: (i, 0)
    out_sds = jax.ShapeDtypeStruct((t, ATT_WIDTH), BF16)
    return pl.pallas_call(
        _inproj_kernel,
        grid=(t // tm,),
        in_specs=[
            pl.BlockSpec((tm, d), tile),
            pl.BlockSpec((1, d), const2),
            pl.BlockSpec((d, in_w), const2),
            pl.BlockSpec((GMLP_GROUPS, GMLP_BLOCK, GMLP_BLOCK), const3),
            pl.BlockSpec((GMLP_GROUPS, GMLP_BLOCK, GMLP_GROUP_DIM), const3),
            pl.BlockSpec((1, GMLP_WIDTH), const2),
            pl.BlockSpec((1, GMLP_WIDTH), const2),
        ],
        out_specs=[pl.BlockSpec((tm, ATT_WIDTH), tile)] * 4,
        out_shape=[out_sds] * 4,
        compiler_params=pltpu.CompilerParams(
            dimension_semantics=("arbitrary",), vmem_limit_bytes=VMEM_LIMIT_BYTES),
        name="inproj_gmlp",
    )(x2, g_mix.reshape(1, d), w_in.astype(BF16), w_s, bsb,
      g_gv.reshape(1, GMLP_WIDTH), g_gout.reshape(1, GMLP_WIDTH))


def _attn_kernel(slopes_ref, lq1_ref, lk1_ref, lq2_ref, lk2_ref, gsub_ref,
                 q_ref, k_ref, v_ref, o_ref, *, qb):
    seq = q_ref.shape[0]
    slope = slopes_ref[pl.program_id(1)]
    lam = (jnp.exp(jnp.sum(lq1_ref[...] * lk1_ref[...], axis=-1, keepdims=True))
           - jnp.exp(jnp.sum(lq2_ref[...] * lk2_ref[...], axis=-1, keepdims=True))
           + LAM_INIT)
    lane = lax.broadcasted_iota(I32, (qb, ATT_VDIM), 1)
    r = lax.broadcasted_iota(I32, (qb, qb), 0)
    c = lax.broadcasted_iota(I32, (qb, qb), 1)
    diag_bias = jnp.where((c // CHUNK) <= (r // CHUNK),
                          slope * (r - jnp.abs(r - c)).astype(F32), -jnp.inf)
    for qi in range(seq // qb):
        off = qi * qb
        qblk = q_ref[off:off + qb, :]
        kd = k_ref[off:off + qb, :]
        bias_d = diag_bias + slope * off
        if off:
            ko = k_ref[0:off, :]
            bias_o = slope * lax.broadcasted_iota(I32, (1, off), 1).astype(F32)

        def softmax_parts(qm):
            sd = lax.dot_general(qm, kd, _NT, preferred_element_type=F32) + bias_d
            m = jnp.max(sd, axis=-1, keepdims=True)
            if off:
                so = lax.dot_general(qm, ko, _NT, preferred_element_type=F32) + bias_o
                m = jnp.maximum(m, jnp.max(so, axis=-1, keepdims=True))
                po = jnp.exp(so - m)
            pd = jnp.exp(sd - m)
            l = jnp.sum(pd, axis=-1, keepdims=True)
            if off:
                return po, pd, l + jnp.sum(po, axis=-1, keepdims=True)
            return None, pd, l

        po0, pd0, l0 = softmax_parts(jnp.where(lane < ATT_HEAD_DIM, qblk, jnp.zeros_like(qblk)))
        po1, pd1, l1 = softmax_parts(jnp.where(lane >= ATT_HEAD_DIM, qblk, jnp.zeros_like(qblk)))
        w0 = 1.0 / l0
        w1 = lam / l1
        o = jnp.dot((pd0 * w0 - pd1 * w1).astype(BF16), v_ref[off:off + qb, :],
                    preferred_element_type=F32)
        if off:
            o += jnp.dot((po0 * w0 - po1 * w1).astype(BF16), v_ref[0:off, :],
                         preferred_element_type=F32)
        o = _rms(o, gsub_ref[...]) * (1.0 - LAM_INIT)
        o_ref[off:off + qb, :] = o.astype(BF16)


def _attn_call(q, k, v, lam_q1, lam_k1, lam_q2, lam_k2, g_subln, batch, seq, qb):
    t = q.shape[0]
    slopes = 2.0 ** (-8.0 * jnp.arange(1, ATT_HEADS + 1, dtype=F32) / ATT_HEADS)
    blk = pl.BlockSpec((seq, ATT_VDIM), lambda b, h: (b, h))
    vec = lambda n: pl.BlockSpec((1, n), lambda b, h: (0, 0))
    return pl.pallas_call(
        functools.partial(_attn_kernel, qb=qb),
        grid=(batch, ATT_HEADS),
        in_specs=[pl.BlockSpec(memory_space=pltpu.SMEM)]
                 + [vec(ATT_HEAD_DIM)] * 4 + [vec(ATT_VDIM)] + [blk] * 3,
        out_specs=blk,
        out_shape=jax.ShapeDtypeStruct((t, ATT_WIDTH), BF16),
        compiler_params=pltpu.CompilerParams(
            dimension_semantics=("arbitrary", "arbitrary"),
            vmem_limit_bytes=VMEM_LIMIT_BYTES),
        name="diff_attn",
    )(slopes, lam_q1.reshape(1, -1), lam_k1.reshape(1, -1), lam_q2.reshape(1, -1),
      lam_k2.reshape(1, -1), g_subln.reshape(1, -1), q, k, v)


def _outproj_kernel(o_ref, y_ref, x_ref, wo_ref, gffn_ref, wq_ref, keys_ref,
                    x1_ref, xn_ref, st_ref):
    x1 = (x_ref[...]
          + jnp.dot(o_ref[...], wo_ref[0:ATT_WIDTH, :], preferred_element_type=F32)
          + jnp.dot(y_ref[...], wo_ref[ATT_WIDTH:, :], preferred_element_type=F32))
    x1_ref[...] = x1
    xn = _rms(x1, gffn_ref[...]).astype(BF16)
    xn_ref[...] = xn
    qp = jnp.dot(xn, wq_ref[...], preferred_element_type=F32).astype(BF16)
    for hp in range(2 * PEER_HEADS):
        st_ref[hp] = lax.dot_general(
            keys_ref[hp], qp[:, hp * PEER_HALF:(hp + 1) * PEER_HALF], _NT,
            preferred_element_type=F32)


def _outproj_call(o, y, x2, w_out, g_ffn, peer_wq, peer_keys, tm):
    t, d = x2.shape
    nq = peer_wq.shape[1]
    tile = lambda i: (i, 0)
    const2 = lambda i: (0, 0)
    keys = peer_keys.reshape(2 * PEER_HEADS, PEER_KEYS, PEER_HALF).astype(BF16)
    return pl.pallas_call(
        _outproj_kernel,
        grid=(t // tm,),
        in_specs=[
            pl.BlockSpec((tm, ATT_WIDTH), tile),
            pl.BlockSpec((tm, GMLP_WIDTH), tile),
            pl.BlockSpec((tm, d), tile),
            pl.BlockSpec((ATT_WIDTH + GMLP_WIDTH, d), const2),
            pl.BlockSpec((1, d), const2),
            pl.BlockSpec((d, nq), const2),
            pl.BlockSpec((2 * PEER_HEADS, PEER_KEYS, PEER_HALF), lambda i: (0, 0, 0)),
        ],
        out_specs=[
            pl.BlockSpec((tm, d), tile),
            pl.BlockSpec((tm, d), tile),
            pl.BlockSpec((2 * PEER_HEADS, PEER_KEYS, tm), lambda i: (0, 0, i)),
        ],
        out_shape=[
            jax.ShapeDtypeStruct((t, d), F32),
            jax.ShapeDtypeStruct((t, d), BF16),
            jax.ShapeDtypeStruct((2 * PEER_HEADS, PEER_KEYS, t), F32),
        ],
        compiler_params=pltpu.CompilerParams(
            dimension_semantics=("arbitrary",), vmem_limit_bytes=VMEM_LIMIT_BYTES),
        name="outproj_scores",
    )(o, y, x2, w_out.astype(BF16), g_ffn.reshape(1, d), peer_wq.astype(BF16), keys)


def _extract_topk(s, ids, k):
    big = jnp.iinfo(jnp.int32).max
    vals, picks = [], []
    for _ in range(k):
        m = jnp.max(s, axis=0, keepdims=True)
        pick = jnp.min(jnp.where(s == m, ids, big), axis=0, keepdims=True)
        vals.append(m)
        picks.append(pick)
        s = jnp.where(ids == pick, -jnp.inf, s)
    return jnp.concatenate(vals, axis=0), jnp.concatenate(picks, axis=0)


def _take_rows(table, sel, n):
    out = jnp.zeros(sel.shape, table.dtype)
    for j in range(n):
        out = jnp.where(sel == j, table[j:j + 1, :], out)
    return out


def _route_kernel(st_ref, a_ref, b_ref, g_ref):
    tt = st_ref.shape[2]
    k = PEER_TOPK
    key_ids = lax.broadcasted_iota(I32, (PEER_KEYS, tt), 0)
    hk = k // 2
    widths = [k] + [hk] * (hk - 1)
    flat = jnp.concatenate(
        [k1 * k + lax.broadcasted_iota(I32, (w, tt), 0) for k1, w in enumerate(widths)]
        + [(hk + lax.broadcasted_iota(I32, (hk, tt), 0)) * k], axis=0)

    def head(h, carry):
        v1, i1 = _extract_topk(st_ref[2 * h], key_ids, k)
        v2, i2 = _extract_topk(st_ref[2 * h + 1], key_ids, k)
        cand = jnp.concatenate(
            [v1[k1:k1 + 1, :] + v2[0:w, :] for k1, w in enumerate(widths)]
            + [v1[hk:k, :] + v2[0:1, :]], axis=0)
        best, pick = _extract_topk(cand, flat, k)
        sel1 = pick // k
        sel2 = pick - sel1 * k
        e = jnp.exp(best - best[0:1, :])
        rows = pl.ds(pl.multiple_of(h * k, k), k)
        a_ref[rows, :] = _take_rows(i1, sel1, k)
        b_ref[rows, :] = _take_rows(i2, sel2, k)
        g_ref[rows, :] = e / jnp.sum(e, axis=0, keepdims=True)
        return carry

    lax.fori_loop(0, PEER_HEADS, head, 0)


def _route_call(st, tt):
    t = st.shape[2]
    hk = PEER_HEADS * PEER_TOPK
    out_spec = pl.BlockSpec((hk, tt), lambda i: (0, i))
    return pl.pallas_call(
        _route_kernel,
        grid=(t // tt,),
        in_specs=[pl.BlockSpec((2 * PEER_HEADS, PEER_KEYS, tt), lambda i: (0, 0, i))],
        out_specs=[out_spec] * 3,
        out_shape=[jax.ShapeDtypeStruct((hk, t), I32),
                   jax.ShapeDtypeStruct((hk, t), I32),
                   jax.ShapeDtypeStruct((hk, t), F32)],
        compiler_params=pltpu.CompilerParams(
            dimension_semantics=("arbitrary",), vmem_limit_bytes=VMEM_LIMIT_BYTES),
        name="peer_route",
    )(st)


W_PITCH = 136
BUILD_UNROLL = 32
HI16 = 0xFFFF0000


def _peer_kernel(xn_ref, x1_ref, at_ref, bt_ref, gt_ref, wut_ref, wv_ref, gfin_ref,
                 out_ref, wbuf, acc, zs, a_s, b_s, g_s, *, cw):
    tb = xn_ref.shape[0]
    c = pl.program_id(1)
    n_chunks = pl.num_programs(1) - 1
    slots = PEER_HEADS * PEER_TOPK

    @pl.when(c == 0)
    def _build_gates():
        acc[...] = jnp.zeros_like(acc)
        zs[1] = jnp.zeros(zs.shape[1:], zs.dtype)
        a_s[...] = at_ref[...].T
        b_s[...] = bt_ref[...].T
        g_s[...] = gt_ref[...].T
        ids = lax.broadcasted_iota(I32, (PEER_KEYS, slots), 0)

        def gate_bits(t):
            arow = a_s[pl.ds(t, 1), :]
            brow = b_s[pl.ds(t, 1), :]
            grow = g_s[pl.ds(t, 1), :]
            ga = jnp.where(ids == arow, grow, 0.0).astype(BF16)
            ob = jnp.where(ids == brow, 1.0, 0.0).astype(BF16)
            wt = lax.dot_general(ga, ob, _NT, preferred_element_type=F32)
            return lax.bitcast_convert_type(wt, jnp.uint32)

        def token_pairs(i, carry):
            for u in range(BUILD_UNROLL // 2):
                p = i * (BUILD_UNROLL // 2) + u
                packed = (gate_bits(2 * p + 1) & jnp.uint32(HI16)) | (gate_bits(2 * p) >> 16)
                wbuf[pl.ds(pl.multiple_of(p * W_PITCH, 8), PEER_KEYS), :] = packed
            return carry

        lax.fori_loop(0, tb // BUILD_UNROLL, token_pairs, 0)

    wr = lax.rem(c, 2)
    cc = jnp.minimum(c, n_chunks - 1)
    hid = jnp.dot(xn_ref[...], wut_ref[...], preferred_element_type=F32)
    for j in range(cw):
        cols = slice(j * PEER_KEYS, (j + 1) * PEER_KEYS)
        w = wbuf[pl.ds(cc * cw + j, tb // 2, stride=W_PITCH), :]
        gate = pltpu.bitcast(w, BF16)
        zs[wr, :, cols] = jax.nn.gelu(hid[:, cols].astype(BF16)) * gate
    acc[...] += jnp.dot(zs[1 - wr], wv_ref[...], preferred_element_type=F32)

    @pl.when(c == n_chunks)
    def _finish():
        out_ref[...] = _rms(x1_ref[...] + acc[...], gfin_ref[...])


def _peer_call(xn, x1, a_t, b_t, g_t, peer_wu, peer_wv, g_final, tb, cw):
    t, d = xn.shape
    n_exp = peer_wu.shape[0]
    hk = PEER_HEADS * PEER_TOPK
    wut = peer_wu.astype(BF16).T
    wv = peer_wv.astype(BF16)
    n_chunks = PEER_KEYS // cw
    tok = lambda i, c: (i, 0)
    tab = pl.BlockSpec((hk, tb), lambda i, c: (0, i))
    return pl.pallas_call(
        functools.partial(_peer_kernel, cw=cw),
        grid=(t // tb, n_chunks + 1),
        in_specs=[
            pl.BlockSpec((tb, d), tok),
            pl.BlockSpec((tb, d), tok),
            tab, tab, tab,
            pl.BlockSpec((d, cw * PEER_KEYS), lambda i, c: (0, jnp.minimum(c, n_chunks - 1))),
            pl.BlockSpec((cw * PEER_KEYS, d), lambda i, c: (jnp.maximum(c - 1, 0), 0)),
            pl.BlockSpec((1, d), lambda i, c: (0, 0)),
        ],
        out_specs=pl.BlockSpec((tb, d), tok),
        out_shape=jax.ShapeDtypeStruct((t, d), F32),
        scratch_shapes=[
            pltpu.VMEM((tb // 2 * W_PITCH, PEER_KEYS), jnp.uint32),
            pltpu.VMEM((tb, d), F32),
            pltpu.VMEM((2, tb, cw * PEER_KEYS), BF16),
            pltpu.VMEM((tb, hk), I32),
            pltpu.VMEM((tb, hk), I32),
            pltpu.VMEM((tb, hk), F32),
        ],
        compiler_params=pltpu.CompilerParams(
            dimension_semantics=("arbitrary", "arbitrary"),
            vmem_limit_bytes=PEER_VMEM_LIMIT_BYTES),
        name="peer_dense",
    )(xn, x1, a_t, b_t, g_t, wut, wv, g_final.reshape(1, d))


def kernel(x, w_in, lam_q1, lam_k1, lam_q2, lam_k2, g_subln, w_s, b_s, g_gv, g_gout,
           w_out, g_mix, g_ffn, peer_wq, peer_keys, peer_wu, peer_wv, g_final):
    batch, seq, d = x.shape
    depth = w_in.shape[0]
    assert depth == 1, "LAM_INIT and the single pass below assume one layer"
    t = batch * seq
    x2 = x.reshape(t, d)
    l = 0
    q, k, v, y = _inproj_call(x2, g_mix[l], w_in[l], w_s[l], b_s[l], g_gv[l], g_gout[l],
                              tm=min(512, t))
    o = _attn_call(q, k, v, lam_q1[l], lam_k1[l], lam_q2[l], lam_k2[l], g_subln[l],
                   batch, seq, qb=min(256, seq))
    x1, xn, st = _outproj_call(o, y, x2, w_out[l], g_ffn[l], peer_wq[l], peer_keys[l],
                               tm=min(512, t))
    a_t, b_t, g_t = _route_call(st, tt=min(256, t))
    out = _peer_call(xn, x1, a_t, b_t, g_t, peer_wu[l], peer_wv[l], g_final,
                     tb=min(512, t), cw=8)
    return out.reshape(batch, seq, d)
```

```python
import functools
import math

import jax
import jax.numpy as jnp
from jax import lax
from jax.experimental import pallas as pl
from jax.experimental.pallas import tpu as pltpu

F32 = jnp.float32
BF16 = jnp.bfloat16
I32 = jnp.int32

EPS = 1e-6
LANES = 128
CHUNK = 64
ATT_HEADS = 4
ATT_HEAD_DIM = 64
ATT_VDIM = 2 * ATT_HEAD_DIM
ATT_WIDTH = ATT_HEADS * ATT_VDIM
GMLP_GROUPS = 4
GMLP_BLOCK = 128
GMLP_GROUP_DIM = 128
GMLP_WIDTH = GMLP_GROUPS * GMLP_GROUP_DIM
PEER_HEADS = 8
PEER_KEYS = 128
KEY_BITS = 7
PEER_TOPK = 16
PEER_HALF = 128
LAM_INIT = 0.8 - 0.6 * math.exp(-0.3 * 0)

VMEM_LIMIT_BYTES = 48 * 1024 * 1024
PEER_VMEM_LIMIT_BYTES = 56 * 1024 * 1024

_NT = (((1,), (1,)), ((), ()))


def _rms(x, g):
    return x * lax.rsqrt(jnp.mean(x * x, axis=-1, keepdims=True) + EPS) * g


def _inproj_kernel(x_ref, gmix_ref, win_ref, ws_ref, bsb_ref, ggv_ref, ggout_ref,
                   q_ref, k_ref, v_ref, y_ref):
    tm = x_ref.shape[0]
    h = _rms(x_ref[...], gmix_ref[...]).astype(BF16)

    def proj(lo, hi):
        return jnp.dot(h, win_ref[:, lo:hi], preferred_element_type=F32)

    aw = ATT_WIDTH
    q_ref[...] = (proj(0, aw) * (ATT_HEAD_DIM ** -0.5)).astype(BF16)
    k_ref[...] = proj(aw, 2 * aw).astype(BF16)
    v_ref[...] = proj(2 * aw, 3 * aw).astype(BF16)
    ug = jax.nn.gelu(proj(3 * aw, 3 * aw + GMLP_WIDTH))
    gg = jax.nn.gelu(proj(3 * aw + GMLP_WIDTH, 3 * aw + 2 * GMLP_WIDTH))

    row = lax.broadcasted_iota(I32, (GMLP_BLOCK, GMLP_BLOCK), 0)
    col = lax.broadcasted_iota(I32, (GMLP_BLOCK, GMLP_BLOCK), 1)
    for g in range(GMLP_GROUPS):
        cs = slice(g * GMLP_GROUP_DIM, (g + 1) * GMLP_GROUP_DIM)
        gvn = _rms(gg[:, cs], ggv_ref[:, cs]).astype(BF16)
        w = jnp.where(row >= col, ws_ref[g], 0.0).astype(BF16)
        for blk in range(tm // GMLP_BLOCK):
            rs = slice(blk * GMLP_BLOCK, (blk + 1) * GMLP_BLOCK)
            gate = jnp.dot(w, gvn[rs], preferred_element_type=F32) + bsb_ref[g]
            y_ref[rs, cs] = _rms(ug[rs, cs] * gate, ggout_ref[:, cs]).astype(BF16)


def _inproj_call(x2, g_mix, w_in, w_s, b_s, g_gv, g_gout, tm):
    t, d = x2.shape
    in_w = w_in.shape[1]
    bsb = jnp.broadcast_to(b_s[:, :, None], (GMLP_GROUPS, GMLP_BLOCK, GMLP_GROUP_DIM))
    const2 = lambda i: (0, 0)
    const3 = lambda i: (0, 0, 0)
    tile = lambda i: (i, 0)
    out_sds = jax.ShapeDtypeStruct((t, ATT_WIDTH), BF16)
    return pl.pallas_call(
        _inproj_kernel,
        grid=(t // tm,),
        in_specs=[
            pl.BlockSpec((tm, d), tile),
            pl.BlockSpec((1, d), const2),
            pl.BlockSpec((d, in_w), const2),
            pl.BlockSpec((GMLP_GROUPS, GMLP_BLOCK, GMLP_BLOCK), const3),
            pl.BlockSpec((GMLP_GROUPS, GMLP_BLOCK, GMLP_GROUP_DIM), const3),
            pl.BlockSpec((1, GMLP_WIDTH), const2),
            pl.BlockSpec((1, GMLP_WIDTH), const2),
        ],
        out_specs=[pl.BlockSpec((tm, ATT_WIDTH), tile)] * 4,
        out_shape=[out_sds] * 4,
        compiler_params=pltpu.CompilerParams(
            dimension_semantics=("arbitrary",), vmem_limit_bytes=VMEM_LIMIT_BYTES),
        name="inproj_gmlp",
    )(x2, g_mix.reshape(1, d), w_in.astype(BF16), w_s, bsb,
      g_gv.reshape(1, GMLP_WIDTH), g_gout.reshape(1, GMLP_WIDTH))


def _attn_kernel(slopes_ref, lq1_ref, lk1_ref, lq2_ref, lk2_ref, gsub_ref,
                 q_ref, k_ref, v_ref, o_ref, *, qb):
    seq = q_ref.shape[0]
    slope = slopes_ref[pl.program_id(1)]
    lam = (jnp.exp(jnp.sum(lq1_ref[...] * lk1_ref[...], axis=-1, keepdims=True))
           - jnp.exp(jnp.sum(lq2_ref[...] * lk2_ref[...], axis=-1, keepdims=True))
           + LAM_INIT)
    lane = lax.broadcasted_iota(I32, (qb, ATT_VDIM), 1)
    r = lax.broadcasted_iota(I32, (qb, qb), 0)
    c = lax.broadcasted_iota(I32, (qb, qb), 1)
    diag_bias = jnp.where((c // CHUNK) <= (r // CHUNK),
                          slope * (r - jnp.abs(r - c)).astype(F32), -jnp.inf)
    for qi in range(seq // qb):
        off = qi * qb
        qblk = q_ref[off:off + qb, :]
        kd = k_ref[off:off + qb, :]
        bias_d = diag_bias + slope * off
        if off:
            ko = k_ref[0:off, :]
            bias_o = slope * lax.broadcasted_iota(I32, (1, off), 1).astype(F32)

        def softmax_parts(qm):
            sd = lax.dot_general(qm, kd, _NT, preferred_element_type=F32) + bias_d
            m = jnp.max(sd, axis=-1, keepdims=True)
            if off:
                so = lax.dot_general(qm, ko, _NT, preferred_element_type=F32) + bias_o
                m = jnp.maximum(m, jnp.max(so, axis=-1, keepdims=True))
                po = jnp.exp(so - m)
            pd = jnp.exp(sd - m)
            l = jnp.sum(pd, axis=-1, keepdims=True)
            if off:
                return po, pd, l + jnp.sum(po, axis=-1, keepdims=True)
            return None, pd, l

        po0, pd0, l0 = softmax_parts(jnp.where(lane < ATT_HEAD_DIM, qblk, jnp.zeros_like(qblk)))
        po1, pd1, l1 = softmax_parts(jnp.where(lane >= ATT_HEAD_DIM, qblk, jnp.zeros_like(qblk)))
        w0 = 1.0 / l0
        w1 = lam / l1
        o = jnp.dot((pd0 * w0 - pd1 * w1).astype(BF16), v_ref[off:off + qb, :],
                    preferred_element_type=F32)
        if off:
            o += jnp.dot((po0 * w0 - po1 * w1).astype(BF16), v_ref[0:off, :],
                         preferred_element_type=F32)
        o = _rms(o, gsub_ref[...]) * (1.0 - LAM_INIT)
        o_ref[off:off + qb, :] = o.astype(BF16)


def _attn_call(q, k, v, lam_q1, lam_k1, lam_q2, lam_k2, g_subln, batch, seq, qb):
    t = q.shape[0]
    slopes = 2.0 ** (-8.0 * jnp.arange(1, ATT_HEADS + 1, dtype=F32) / ATT_HEADS)
    blk = pl.BlockSpec((seq, ATT_VDIM), lambda b, h: (b, h))
    vec = lambda n: pl.BlockSpec((1, n), lambda b, h: (0, 0))
    return pl.pallas_call(
        functools.partial(_attn_kernel, qb=qb),
        grid=(batch, ATT_HEADS),
        in_specs=[pl.BlockSpec(memory_space=pltpu.SMEM)]
                 + [vec(ATT_HEAD_DIM)] * 4 + [vec(ATT_VDIM)] + [blk] * 3,
        out_specs=blk,
        out_shape=jax.ShapeDtypeStruct((t, ATT_WIDTH), BF16),
        compiler_params=pltpu.CompilerParams(
            dimension_semantics=("arbitrary", "arbitrary"),
            vmem_limit_bytes=VMEM_LIMIT_BYTES),
        name="diff_attn",
    )(slopes, lam_q1.reshape(1, -1), lam_k1.reshape(1, -1), lam_q2.reshape(1, -1),
      lam_k2.reshape(1, -1), g_subln.reshape(1, -1), q, k, v)


def _outproj_kernel(o_ref, y_ref, x_ref, wo_ref, gffn_ref, wq_ref, keys_ref,
                    x1_ref, xn_ref, st_ref):
    x1 = (x_ref[...]
          + jnp.dot(o_ref[...], wo_ref[0:ATT_WIDTH, :], preferred_element_type=F32)
          + jnp.dot(y_ref[...], wo_ref[ATT_WIDTH:, :], preferred_element_type=F32))
    x1_ref[...] = x1
    xn = _rms(x1, gffn_ref[...]).astype(BF16)
    xn_ref[...] = xn
    qp = jnp.dot(xn, wq_ref[...], preferred_element_type=F32).astype(BF16)
    for hp in range(2 * PEER_HEADS):
        st_ref[hp] = lax.dot_general(
            keys_ref[hp], qp[:, hp * PEER_HALF:(hp + 1) * PEER_HALF], _NT,
            preferred_element_type=F32)


def _outproj_call(o, y, x2, w_out, g_ffn, peer_wq, peer_keys, tm):
    t, d = x2.shape
    nq = peer_wq.shape[1]
    tile = lambda i: (i, 0)
    const2 = lambda i: (0, 0)
    keys = peer_keys.reshape(2 * PEER_HEADS, PEER_KEYS, PEER_HALF).astype(BF16)
    return pl.pallas_call(
        _outproj_kernel,
        grid=(t // tm,),
        in_specs=[
            pl.BlockSpec((tm, ATT_WIDTH), tile),
            pl.BlockSpec((tm, GMLP_WIDTH), tile),
            pl.BlockSpec((tm, d), tile),
            pl.BlockSpec((ATT_WIDTH + GMLP_WIDTH, d), const2),
            pl.BlockSpec((1, d), const2),
            pl.BlockSpec((d, nq), const2),
            pl.BlockSpec((2 * PEER_HEADS, PEER_KEYS, PEER_HALF), lambda i: (0, 0, 0)),
        ],
        out_specs=[
            pl.BlockSpec((tm, d), tile),
            pl.BlockSpec((tm, d), tile),
            pl.BlockSpec((2 * PEER_HEADS, PEER_KEYS, tm), lambda i: (0, 0, i)),
        ],
        out_shape=[
            jax.ShapeDtypeStruct((t, d), F32),
            jax.ShapeDtypeStruct((t, d), BF16),
            jax.ShapeDtypeStruct((2 * PEER_HEADS, PEER_KEYS, t), F32),
        ],
        compiler_params=pltpu.CompilerParams(
            dimension_semantics=("arbitrary",), vmem_limit_bytes=VMEM_LIMIT_BYTES),
        name="outproj_scores",
    )(o, y, x2, w_out.astype(BF16), g_ffn.reshape(1, d), peer_wq.astype(BF16), keys)


def _extract_topk(s, ids, k):
    big = jnp.iinfo(jnp.int32).max
    vals, picks = [], []
    for _ in range(k):
        m = jnp.max(s, axis=0, keepdims=True)
        pick = jnp.min(jnp.where(s == m, ids, big), axis=0, keepdims=True)
        vals.append(m)
        picks.append(pick)
        s = jnp.where(ids == pick, -jnp.inf, s)
    return jnp.concatenate(vals, axis=0), jnp.concatenate(picks, axis=0)


def _take_rows(table, sel, n):
    out = jnp.zeros(sel.shape, table.dtype)
    for j in range(n):
        out = jnp.where(sel == j, table[j:j + 1], out)
    return out


def _sort_network(n):
    pairs = []

    def merge(lo, length, r):
        step = 2 * r
        if step < length:
            merge(lo, length, step)
            merge(lo + r, length, step)
            pairs.extend((i, i + r) for i in range(lo + r, lo + length - r, step))
        else:
            pairs.append((lo, lo + r))

    def sort(lo, length):
        if length > 1:
            sort(lo, length // 2)
            sort(lo + length // 2, length // 2)
            merge(lo, length, 1)

    sort(0, n)
    return pairs


def _compare_exchange(vals, ids, i, j):
    a, b = vals[i], vals[j]
    ge = a >= b
    vals[i], vals[j] = jnp.maximum(a, b), jnp.minimum(a, b)
    ids[i], ids[j] = jnp.where(ge, ids[i], ids[j]), jnp.where(ge, ids[j], ids[i])


def _bitonic_merge(vals, ids):
    n = len(vals)
    d = n // 2
    while d:
        for i in range(n):
            if (i // d) % 2 == 0:
                _compare_exchange(vals, ids, i, i + d)
        d //= 2


def _merge_top(av, ai, bv, bi):
    n = len(av)
    cv, ci = [], []
    for r in range(n):
        a, b = av[r], bv[n - 1 - r]
        cv.append(jnp.maximum(a, b))
        ci.append(jnp.where(a >= b, ai[r], bi[n - 1 - r]))
    _bitonic_merge(cv, ci)
    return cv, ci


def _merge_all(av, ai, bv, bi):
    cv, ci = av + bv[::-1], ai + bi[::-1]
    _bitonic_merge(cv, ci)
    return cv, ci


def _tie_flag(rows, top, k):
    flag = jnp.zeros(top[0].shape, I32)
    for r in range(k - 1):
        flag = jnp.where(top[r] == top[r + 1], 1, flag)
    count = jnp.zeros(top[0].shape, I32)
    for row in rows:
        count = count + jnp.where(row >= top[k - 1], 1, 0)
    return jnp.where(count != k, 1, flag)


def _net_topk(rows, k):
    pairs = _sort_network(k)
    lists = []
    for g in range(0, len(rows), k):
        vals = list(rows[g:g + k])
        ids = [jnp.full(rows[0].shape, g + j, I32) for j in range(k)]
        for i, j in pairs:
            _compare_exchange(vals, ids, i, j)
        lists.append((vals, ids))
    while len(lists) > 1:
        lists = [_merge_top(*lists[m], *lists[m + 1]) for m in range(0, len(lists), 2)]
    vals, ids = lists[0]
    return vals, ids, _tie_flag(rows, vals, k)


def _route_kernel(st_ref, a_ref, b_ref, g_ref):
    k = PEER_TOPK
    hk = k // 2
    shape = st_ref.shape[3:]
    widths = [k] + [hk] * (hk - 1)

    def finish(best, expert):
        e = jnp.exp(best - best[0:1])
        a_ref[0] = expert >> KEY_BITS
        b_ref[0] = expert & (PEER_KEYS - 1)
        g_ref[0] = e / jnp.sum(e, axis=0, keepdims=True)

    v1, i1, f1 = _net_topk([st_ref[0, 0, n] for n in range(PEER_KEYS)], k)
    v2, i2, f2 = _net_topk([st_ref[0, 1, n] for n in range(PEER_KEYS)], k)
    base = [i * PEER_KEYS for i in i1]
    groups = [([v1[k1] + v2[k2] for k2 in range(w)], [base[k1] + i2[k2] for k2 in range(w)])
              for k1, w in enumerate(widths)]
    groups.append(([v1[k1] + v2[0] for k1 in range(hk, k)], [base[k1] + i2[0] for k1 in range(hk, k)]))
    cand_rows = [v for g in groups for v in g[0]]
    full = [groups[0]] + [_merge_all(*groups[m], *groups[m + 1]) for m in range(1, len(groups), 2)]
    while len(full) > 1:
        nxt = [_merge_top(*full[m], *full[m + 1]) for m in range(0, len(full) - 1, 2)]
        full = nxt + full[len(full) - len(full) % 2:]
    best, expert = full[0]
    finish(jnp.stack(best), jnp.stack(expert))
    tie = jnp.maximum(jnp.maximum(f1, f2), _tie_flag(cand_rows, best, k))

    @pl.when(jnp.max(tie) > 0)
    def _exact():
        key_ids = lax.broadcasted_iota(I32, (PEER_KEYS,) + shape, 0)
        flat = jnp.concatenate(
            [k1 * k + lax.broadcasted_iota(I32, (w,) + shape, 0) for k1, w in enumerate(widths)]
            + [(hk + lax.broadcasted_iota(I32, (hk,) + shape, 0)) * k], axis=0)
        xv1, xi1 = _extract_topk(st_ref[0, 0], key_ids, k)
        xv2, xi2 = _extract_topk(st_ref[0, 1], key_ids, k)
        cand = jnp.concatenate(
            [xv1[k1:k1 + 1] + xv2[0:w] for k1, w in enumerate(widths)]
            + [xv1[hk:k] + xv2[0:1]], axis=0)
        xbest, pick = _extract_topk(cand, flat, k)
        sel1 = pick // k
        sel2 = pick - sel1 * k
        finish(xbest, _take_rows(xi1, sel1, k) * PEER_KEYS + _take_rows(xi2, sel2, k))


ROUTE_TOKENS = 8 * LANES


def _route_call(st):
    t = st.shape[2]
    groups = t // LANES
    st5 = st.reshape(PEER_HEADS, 2, PEER_KEYS, groups, LANES)
    out_spec = pl.BlockSpec((1, PEER_TOPK, 8, LANES), lambda i, h: (h, 0, i, 0))
    out_sds = lambda dt: jax.ShapeDtypeStruct((PEER_HEADS, PEER_TOPK, groups, LANES), dt)
    a, b, g = pl.pallas_call(
        _route_kernel,
        grid=(t // ROUTE_TOKENS, PEER_HEADS),
        in_specs=[pl.BlockSpec((1, 2, PEER_KEYS, 8, LANES), lambda i, h: (h, 0, 0, i, 0))],
        out_specs=[out_spec] * 3,
        out_shape=[out_sds(I32), out_sds(I32), out_sds(F32)],
        compiler_params=pltpu.CompilerParams(
            dimension_semantics=("arbitrary", "arbitrary"),
            vmem_limit_bytes=VMEM_LIMIT_BYTES),
        name="peer_route",
    )(st5)
    slots = PEER_HEADS * PEER_TOPK
    return a.reshape(slots, t), b.reshape(slots, t), g.reshape(slots, t)


W_PITCH = 136
BUILD_UNROLL = 32
HI16 = 0xFFFF0000


def _peer_kernel(xn_ref, x1_ref, at_ref, bt_ref, gt_ref, wut_ref, wv_ref, gfin_ref,
                 out_ref, wbuf, acc, zs, a_s, b_s, g_s, *, cw):
    tb = xn_ref.shape[0]
    c = pl.program_id(1)
    n_chunks = pl.num_programs(1) - 1
    slots = PEER_HEADS * PEER_TOPK

    @pl.when(c == 0)
    def _build_gates():
        acc[...] = jnp.zeros_like(acc)
        zs[1] = jnp.zeros(zs.shape[1:], zs.dtype)
        a_s[...] = at_ref[...].T
        b_s[...] = bt_ref[...].T
        g_s[...] = gt_ref[...].T
        ids = lax.broadcasted_iota(I32, (PEER_KEYS, slots), 0)

        def gate_bits(t):
            arow = a_s[pl.ds(t, 1), :]
            brow = b_s[pl.ds(t, 1), :]
            grow = g_s[pl.ds(t, 1), :]
            ga = jnp.where(ids == arow, grow, 0.0).astype(BF16)
            ob = jnp.where(ids == brow, 1.0, 0.0).astype(BF16)
            wt = lax.dot_general(ga, ob, _NT, preferred_element_type=F32)
            return lax.bitcast_convert_type(wt, jnp.uint32)

        def token_pairs(i, carry):
            for u in range(BUILD_UNROLL // 2):
                p = i * (BUILD_UNROLL // 2) + u
                packed = (gate_bits(2 * p + 1) & jnp.uint32(HI16)) | (gate_bits(2 * p) >> 16)
                wbuf[pl.ds(pl.multiple_of(p * W_PITCH, 8), PEER_KEYS), :] = packed
            return carry

        lax.fori_loop(0, tb // BUILD_UNROLL, token_pairs, 0)

    wr = lax.rem(c, 2)
    cc = jnp.minimum(c, n_chunks - 1)
    hid = jnp.dot(xn_ref[...], wut_ref[...], preferred_element_type=F32)
    for j in range(cw):
        cols = slice(j * PEER_KEYS, (j + 1) * PEER_KEYS)
        w = wbuf[pl.ds(cc * cw + j, tb // 2, stride=W_PITCH), :]
        gate = pltpu.bitcast(w, BF16)
        zs[wr, :, cols] = jax.nn.gelu(hid[:, cols].astype(BF16)) * gate
    acc[...] += jnp.dot(zs[1 - wr], wv_ref[...], preferred_element_type=F32)

    @pl.when(c == n_chunks)
    def _finish():
        out_ref[...] = _rms(x1_ref[...] + acc[...], gfin_ref[...])


def _peer_call(xn, x1, a_t, b_t, g_t, peer_wu, peer_wv, g_final, tb, cw):
    t, d = xn.shape
    n_exp = peer_wu.shape[0]
    hk = PEER_HEADS * PEER_TOPK
    wut = peer_wu.astype(BF16).T
    wv = peer_wv.astype(BF16)
    n_chunks = PEER_KEYS // cw
    tok = lambda i, c: (i, 0)
    tab = pl.BlockSpec((hk, tb), lambda i, c: (0, i))
    return pl.pallas_call(
        functools.partial(_peer_kernel, cw=cw),
        grid=(t // tb, n_chunks + 1),
        in_specs=[
            pl.BlockSpec((tb, d), tok),
            pl.BlockSpec((tb, d), tok),
            tab, tab, tab,
            pl.BlockSpec((d, cw * PEER_KEYS), lambda i, c: (0, jnp.minimum(c, n_chunks - 1))),
            pl.BlockSpec((cw * PEER_KEYS, d), lambda i, c: (jnp.maximum(c - 1, 0), 0)),
            pl.BlockSpec((1, d), lambda i, c: (0, 0)),
        ],
        out_specs=pl.BlockSpec((tb, d), tok),
        out_shape=jax.ShapeDtypeStruct((t, d), F32),
        scratch_shapes=[
            pltpu.VMEM((tb // 2 * W_PITCH, PEER_KEYS), jnp.uint32),
            pltpu.VMEM((tb, d), F32),
            pltpu.VMEM((2, tb, cw * PEER_KEYS), BF16),
            pltpu.VMEM((tb, hk), I32),
            pltpu.VMEM((tb, hk), I32),
            pltpu.VMEM((tb, hk), F32),
        ],
        compiler_params=pltpu.CompilerParams(
            dimension_semantics=("arbitrary", "arbitrary"),
            vmem_limit_bytes=PEER_VMEM_LIMIT_BYTES),
        name="peer_dense",
    )(xn, x1, a_t, b_t, g_t, wut, wv, g_final.reshape(1, d))


def kernel(x, w_in, lam_q1, lam_k1, lam_q2, lam_k2, g_subln, w_s, b_s, g_gv, g_gout,
           w_out, g_mix, g_ffn, peer_wq, peer_keys, peer_wu, peer_wv, g_final):
    batch, seq, d = x.shape
    depth = w_in.shape[0]
    assert depth == 1, "LAM_INIT and the single pass below assume one layer"
    t = batch * seq
    x2 = x.reshape(t, d)
    l = 0
    q, k, v, y = _inproj_call(x2, g_mix[l], w_in[l], w_s[l], b_s[l], g_gv[l], g_gout[l],
                              tm=min(512, t))
    o = _attn_call(q, k, v, lam_q1[l], lam_k1[l], lam_q2[l], lam_k2[l], g_subln[l],
                   batch, seq, qb=min(256, seq))
    x1, xn, st = _outproj_call(o, y, x2, w_out[l], g_ffn[l], peer_wq[l], peer_keys[l],
                               tm=min(512, t))
    a_t, b_t, g_t = _route_call(st)
    out = _peer_call(xn, x1, a_t, b_t, g_t, peer_wu[l], peer_wv[l], g_final,
                     tb=min(512, t), cw=8)
    return out.reshape(batch, seq, d)
```

```python
import functools
import math

import jax
import jax.numpy as jnp
from jax import lax
from jax.experimental import pallas as pl
from jax.experimental.pallas import tpu as pltpu

F32 = jnp.float32
BF16 = jnp.bfloat16
I32 = jnp.int32

EPS = 1e-6
LANES = 128
CHUNK = 64
ATT_HEADS = 4
ATT_HEAD_DIM = 64
ATT_VDIM = 2 * ATT_HEAD_DIM
ATT_WIDTH = ATT_HEADS * ATT_VDIM
GMLP_GROUPS = 4
GMLP_BLOCK = 128
GMLP_GROUP_DIM = 128
GMLP_WIDTH = GMLP_GROUPS * GMLP_GROUP_DIM
PEER_HEADS = 8
PEER_KEYS = 128
KEY_BITS = 7
PEER_TOPK = 16
PEER_HALF = 128
LAM_INIT = 0.8 - 0.6 * math.exp(-0.3 * 0)
ROUTE_TOKENS = 8 * LANES

VMEM_LIMIT_BYTES = 48 * 1024 * 1024
PEER_VMEM_LIMIT_BYTES = 56 * 1024 * 1024

_NT = (((1,), (1,)), ((), ()))


def _rms(x, g):
    return x * lax.rsqrt(jnp.mean(x * x, axis=-1, keepdims=True) + EPS) * g


def _inproj_kernel(x_ref, gmix_ref, win_ref, ws_ref, bsb_ref, ggv_ref, ggout_ref,
                   q_ref, k_ref, v_ref, y_ref):
    tm = x_ref.shape[0]
    h = _rms(x_ref[...], gmix_ref[...]).astype(BF16)

    def proj(lo, hi):
        return jnp.dot(h, win_ref[:, lo:hi], preferred_element_type=F32)

    aw = ATT_WIDTH
    q_ref[...] = (proj(0, aw) * (ATT_HEAD_DIM ** -0.5)).astype(BF16)
    k_ref[...] = proj(aw, 2 * aw).astype(BF16)
    v_ref[...] = proj(2 * aw, 3 * aw).astype(BF16)
    ug = jax.nn.gelu(proj(3 * aw, 3 * aw + GMLP_WIDTH))
    gg = jax.nn.gelu(proj(3 * aw + GMLP_WIDTH, 3 * aw + 2 * GMLP_WIDTH))

    row = lax.broadcasted_iota(I32, (GMLP_BLOCK, GMLP_BLOCK), 0)
    col = lax.broadcasted_iota(I32, (GMLP_BLOCK, GMLP_BLOCK), 1)
    for g in range(GMLP_GROUPS):
        cs = slice(g * GMLP_GROUP_DIM, (g + 1) * GMLP_GROUP_DIM)
        gvn = _rms(gg[:, cs], ggv_ref[:, cs]).astype(BF16)
        w = jnp.where(row >= col, ws_ref[g], 0.0).astype(BF16)
        for blk in range(tm // GMLP_BLOCK):
            rs = slice(blk * GMLP_BLOCK, (blk + 1) * GMLP_BLOCK)
            gate = jnp.dot(w, gvn[rs], preferred_element_type=F32) + bsb_ref[g]
            y_ref[rs, cs] = _rms(ug[rs, cs] * gate, ggout_ref[:, cs]).astype(BF16)


def _inproj_call(x2, g_mix, w_in, w_s, b_s, g_gv, g_gout, tm):
    t, d = x2.shape
    in_w = w_in.shape[1]
    bsb = jnp.broadcast_to(b_s[:, :, None], (GMLP_GROUPS, GMLP_BLOCK, GMLP_GROUP_DIM))
    const2 = lambda i: (0, 0)
    const3 = lambda i: (0, 0, 0)
    tile = lambda i: (i, 0)
    out_sds = jax.ShapeDtypeStruct((t, ATT_WIDTH), BF16)
    return pl.pallas_call(
        _inproj_kernel,
        grid=(t // tm,),
        in_specs=[
            pl.BlockSpec((tm, d), tile),
            pl.BlockSpec((1, d), const2),
            pl.BlockSpec((d, in_w), const2),
            pl.BlockSpec((GMLP_GROUPS, GMLP_BLOCK, GMLP_BLOCK), const3),
            pl.BlockSpec((GMLP_GROUPS, GMLP_BLOCK, GMLP_GROUP_DIM), const3),
            pl.BlockSpec((1, GMLP_WIDTH), const2),
            pl.BlockSpec((1, GMLP_WIDTH), const2),
        ],
        out_specs=[pl.BlockSpec((tm, ATT_WIDTH), tile)] * 4,
        out_shape=[out_sds] * 4,
        compiler_params=pltpu.CompilerParams(
            dimension_semantics=("arbitrary",), vmem_limit_bytes=VMEM_LIMIT_BYTES),
        name="inproj_gmlp",
    )(x2, g_mix.reshape(1, d), w_in.astype(BF16), w_s, bsb,
      g_gv.reshape(1, GMLP_WIDTH), g_gout.reshape(1, GMLP_WIDTH))


def _attn_kernel(slopes_ref, lq1_ref, lk1_ref, lq2_ref, lk2_ref, gsub_ref,
                 q_ref, k_ref, v_ref, o_ref, *, qb):
    seq = q_ref.shape[0]
    slope = slopes_ref[pl.program_id(1)]
    lam = (jnp.exp(jnp.sum(lq1_ref[...] * lk1_ref[...], axis=-1, keepdims=True))
           - jnp.exp(jnp.sum(lq2_ref[...] * lk2_ref[...], axis=-1, keepdims=True))
           + LAM_INIT)
    lane = lax.broadcasted_iota(I32, (qb, ATT_VDIM), 1)
    r = lax.broadcasted_iota(I32, (qb, qb), 0)
    c = lax.broadcasted_iota(I32, (qb, qb), 1)
    diag_bias = jnp.where((c // CHUNK) <= (r // CHUNK),
                          slope * (r - jnp.abs(r - c)).astype(F32), -jnp.inf)
    for qi in range(seq // qb):
        off = qi * qb
        qblk = q_ref[off:off + qb, :]
        kd = k_ref[off:off + qb, :]
        bias_d = diag_bias + slope * off
        if off:
            ko = k_ref[0:off, :]
            bias_o = slope * lax.broadcasted_iota(I32, (1, off), 1).astype(F32)

        def softmax_parts(qm):
            sd = lax.dot_general(qm, kd, _NT, preferred_element_type=F32) + bias_d
            m = jnp.max(sd, axis=-1, keepdims=True)
            if off:
                so = lax.dot_general(qm, ko, _NT, preferred_element_type=F32) + bias_o
                m = jnp.maximum(m, jnp.max(so, axis=-1, keepdims=True))
                po = jnp.exp(so - m)
            pd = jnp.exp(sd - m)
            l = jnp.sum(pd, axis=-1, keepdims=True)
            if off:
                return po, pd, l + jnp.sum(po, axis=-1, keepdims=True)
            return None, pd, l

        po0, pd0, l0 = softmax_parts(jnp.where(lane < ATT_HEAD_DIM, qblk, jnp.zeros_like(qblk)))
        po1, pd1, l1 = softmax_parts(jnp.where(lane >= ATT_HEAD_DIM, qblk, jnp.zeros_like(qblk)))
        w0 = 1.0 / l0
        w1 = lam / l1
        o = jnp.dot((pd0 * w0 - pd1 * w1).astype(BF16), v_ref[off:off + qb, :],
                    preferred_element_type=F32)
        if off:
            o += jnp.dot((po0 * w0 - po1 * w1).astype(BF16), v_ref[0:off, :],
                         preferred_element_type=F32)
        o = _rms(o, gsub_ref[...]) * (1.0 - LAM_INIT)
        o_ref[off:off + qb, :] = o.astype(BF16)


def _attn_call(q, k, v, lam_q1, lam_k1, lam_q2, lam_k2, g_subln, batch, seq, qb):
    t = q.shape[0]
    slopes = 2.0 ** (-8.0 * jnp.arange(1, ATT_HEADS + 1, dtype=F32) / ATT_HEADS)
    blk = pl.BlockSpec((seq, ATT_VDIM), lambda b, h: (b, h))
    vec = lambda n: pl.BlockSpec((1, n), lambda b, h: (0, 0))
    return pl.pallas_call(
        functools.partial(_attn_kernel, qb=qb),
        grid=(batch, ATT_HEADS),
        in_specs=[pl.BlockSpec(memory_space=pltpu.SMEM)]
                 + [vec(ATT_HEAD_DIM)] * 4 + [vec(ATT_VDIM)] + [blk] * 3,
        out_specs=blk,
        out_shape=jax.ShapeDtypeStruct((t, ATT_WIDTH), BF16),
        compiler_params=pltpu.CompilerParams(
            dimension_semantics=("arbitrary", "arbitrary"),
            vmem_limit_bytes=VMEM_LIMIT_BYTES),
        name="diff_attn",
    )(slopes, lam_q1.reshape(1, -1), lam_k1.reshape(1, -1), lam_q2.reshape(1, -1),
      lam_k2.reshape(1, -1), g_subln.reshape(1, -1), q, k, v)


def _outproj_kernel(o_ref, y_ref, x_ref, wo_ref, gffn_ref, wq_ref, keys_ref,
                    x1_ref, xn_ref, st_ref):
    x1 = (x_ref[...]
          + jnp.dot(o_ref[...], wo_ref[0:ATT_WIDTH, :], preferred_element_type=F32)
          + jnp.dot(y_ref[...], wo_ref[ATT_WIDTH:, :], preferred_element_type=F32))
    x1_ref[...] = x1
    xn = _rms(x1, gffn_ref[...]).astype(BF16)
    xn_ref[...] = xn
    qp = jnp.dot(xn, wq_ref[...], preferred_element_type=F32).astype(BF16)
    groups = st_ref.shape[4]
    for hp in range(2 * PEER_HEADS):
        s = lax.dot_general(
            keys_ref[hp], qp[:, hp * PEER_HALF:(hp + 1) * PEER_HALF], _NT,
            preferred_element_type=F32)
        for g in range(groups):
            st_ref[hp // 2, hp % 2, 0, :, g] = s[:, g * LANES:(g + 1) * LANES].reshape(
                PEER_KEYS // 8, 8, LANES)


def _outproj_call(o, y, x2, w_out, g_ffn, peer_wq, peer_keys, tm):
    t, d = x2.shape
    nq = peer_wq.shape[1]
    tile = lambda i: (i, 0)
    const2 = lambda i: (0, 0)
    keys = peer_keys.reshape(2 * PEER_HEADS, PEER_KEYS, PEER_HALF).astype(BF16)
    assert ROUTE_TOKENS % tm == 0 and t % ROUTE_TOKENS == 0
    steps_per_tile = ROUTE_TOKENS // tm
    return pl.pallas_call(
        _outproj_kernel,
        grid=(t // tm,),
        in_specs=[
            pl.BlockSpec((tm, ATT_WIDTH), tile),
            pl.BlockSpec((tm, GMLP_WIDTH), tile),
            pl.BlockSpec((tm, d), tile),
            pl.BlockSpec((ATT_WIDTH + GMLP_WIDTH, d), const2),
            pl.BlockSpec((1, d), const2),
            pl.BlockSpec((d, nq), const2),
            pl.BlockSpec((2 * PEER_HEADS, PEER_KEYS, PEER_HALF), lambda i: (0, 0, 0)),
        ],
        out_specs=[
            pl.BlockSpec((tm, d), tile),
            pl.BlockSpec((tm, d), tile),
            pl.BlockSpec((PEER_HEADS, 2, 1, PEER_KEYS // 8, tm // LANES, 8, LANES),
                         lambda i: (0, 0, i // steps_per_tile, 0, i % steps_per_tile, 0, 0)),
        ],
        out_shape=[
            jax.ShapeDtypeStruct((t, d), F32),
            jax.ShapeDtypeStruct((t, d), BF16),
            jax.ShapeDtypeStruct((PEER_HEADS, 2, t // ROUTE_TOKENS, PEER_KEYS // 8, 8, 8, LANES), F32),
        ],
        compiler_params=pltpu.CompilerParams(
            dimension_semantics=("arbitrary",), vmem_limit_bytes=VMEM_LIMIT_BYTES),
        name="outproj_scores",
    )(o, y, x2, w_out.astype(BF16), g_ffn.reshape(1, d), peer_wq.astype(BF16), keys)


def _extract_topk(s, ids, k):
    big = jnp.iinfo(jnp.int32).max
    vals, picks = [], []
    for _ in range(k):
        m = jnp.max(s, axis=0, keepdims=True)
        pick = jnp.min(jnp.where(s == m, ids, big), axis=0, keepdims=True)
        vals.append(m)
        picks.append(pick)
        s = jnp.where(ids == pick, -jnp.inf, s)
    return jnp.concatenate(vals, axis=0), jnp.concatenate(picks, axis=0)


def _take_rows(table, sel, n):
    out = jnp.zeros(sel.shape, table.dtype)
    for j in range(n):
        out = jnp.where(sel == j, table[j:j + 1], out)
    return out


def _sort_network(n):
    pairs = []

    def merge(lo, length, r):
        step = 2 * r
        if step < length:
            merge(lo, length, step)
            merge(lo + r, length, step)
            pairs.extend((i, i + r) for i in range(lo + r, lo + length - r, step))
        else:
            pairs.append((lo, lo + r))

    def sort(lo, length):
        if length > 1:
            sort(lo, length // 2)
            sort(lo + length // 2, length // 2)
            merge(lo, length, 1)

    sort(0, n)
    return pairs


def _compare_exchange(vals, ids, i, j):
    a, b = vals[i], vals[j]
    ge = a >= b
    vals[i], vals[j] = jnp.maximum(a, b), jnp.minimum(a, b)
    ids[i], ids[j] = jnp.where(ge, ids[i], ids[j]), jnp.where(ge, ids[j], ids[i])


def _bitonic_merge(vals, ids):
    n = len(vals)
    d = n // 2
    while d:
        for i in range(n):
            if (i // d) % 2 == 0:
                _compare_exchange(vals, ids, i, i + d)
        d //= 2


def _merge_top(av, ai, bv, bi):
    n = len(av)
    cv, ci = [], []
    for r in range(n):
        a, b = av[r], bv[n - 1 - r]
        cv.append(jnp.maximum(a, b))
        ci.append(jnp.where(a >= b, ai[r], bi[n - 1 - r]))
    _bitonic_merge(cv, ci)
    return cv, ci


def _merge_all(av, ai, bv, bi):
    cv, ci = av + bv[::-1], ai + bi[::-1]
    _bitonic_merge(cv, ci)
    return cv, ci


def _tie_flag(rows, top, k):
    flag = jnp.zeros(top[0].shape, I32)
    for r in range(k - 1):
        flag = jnp.where(top[r] == top[r + 1], 1, flag)
    count = jnp.zeros(top[0].shape, I32)
    for row in rows:
        count = count + jnp.where(row >= top[k - 1], 1, 0)
    return jnp.where(count != k, 1, flag)


def _net_topk(rows, k):
    pairs = _sort_network(k)
    lists = []
    for g in range(0, len(rows), k):
        vals = list(rows[g:g + k])
        ids = [jnp.full(rows[0].shape, g + j, I32) for j in range(k)]
        for i, j in pairs:
            _compare_exchange(vals, ids, i, j)
        lists.append((vals, ids))
    while len(lists) > 1:
        lists = [_merge_top(*lists[m], *lists[m + 1]) for m in range(0, len(lists), 2)]
    vals, ids = lists[0]
    return vals, ids, _tie_flag(rows, vals, k)


def _route_kernel(st_ref, a_ref, b_ref, g_ref):
    k = PEER_TOPK
    hk = k // 2
    shape = (8, LANES)

    def key_rows(side):
        return [st_ref[0, side, 0, pl.ds((n // 8) * 64 + n % 8, 8, stride=8), :]
                for n in range(PEER_KEYS)]
    widths = [k] + [hk] * (hk - 1)

    def finish(best, expert):
        e = jnp.exp(best - best[0:1])
        a_ref[0] = expert >> KEY_BITS
        b_ref[0] = expert & (PEER_KEYS - 1)
        g_ref[0] = e / jnp.sum(e, axis=0, keepdims=True)

    v1, i1, f1 = _net_topk(key_rows(0), k)
    v2, i2, f2 = _net_topk(key_rows(1), k)
    base = [i * PEER_KEYS for i in i1]
    groups = [([v1[k1] + v2[k2] for k2 in range(w)], [base[k1] + i2[k2] for k2 in range(w)])
              for k1, w in enumerate(widths)]
    groups.append(([v1[k1] + v2[0] for k1 in range(hk, k)], [base[k1] + i2[0] for k1 in range(hk, k)]))
    cand_rows = [v for g in groups for v in g[0]]
    full = [groups[0]] + [_merge_all(*groups[m], *groups[m + 1]) for m in range(1, len(groups), 2)]
    while len(full) > 1:
        nxt = [_merge_top(*full[m], *full[m + 1]) for m in range(0, len(full) - 1, 2)]
        full = nxt + full[len(full) - len(full) % 2:]
    best, expert = full[0]
    finish(jnp.stack(best), jnp.stack(expert))
    tie = jnp.maximum(jnp.maximum(f1, f2), _tie_flag(cand_rows, best, k))

    @pl.when(jnp.max(tie) > 0)
    def _exact():
        key_ids = lax.broadcasted_iota(I32, (PEER_KEYS,) + shape, 0)
        flat = jnp.concatenate(
            [k1 * k + lax.broadcasted_iota(I32, (w,) + shape, 0) for k1, w in enumerate(widths)]
            + [(hk + lax.broadcasted_iota(I32, (hk,) + shape, 0)) * k], axis=0)
        xv1, xi1 = _extract_topk(jnp.stack(key_rows(0)), key_ids, k)
        xv2, xi2 = _extract_topk(jnp.stack(key_rows(1)), key_ids, k)
        cand = jnp.concatenate(
            [xv1[k1:k1 + 1] + xv2[0:w] for k1, w in enumerate(widths)]
            + [xv1[hk:k] + xv2[0:1]], axis=0)
        xbest, pick = _extract_topk(cand, flat, k)
        sel1 = pick // k
        sel2 = pick - sel1 * k
        finish(xbest, _take_rows(xi1, sel1, k) * PEER_KEYS + _take_rows(xi2, sel2, k))


def _route_call(st):
    tiles = st.shape[2]
    slots = PEER_HEADS * PEER_TOPK
    st = st.reshape(PEER_HEADS, 2, tiles, PEER_KEYS * 8, LANES)
    out_spec = pl.BlockSpec((1, PEER_TOPK, 8, LANES), lambda i, h: (i, h, 0, 0))
    out_sds = lambda dt: jax.ShapeDtypeStruct((tiles, slots, 8, LANES), dt)
    outs = pl.pallas_call(
        _route_kernel,
        grid=(tiles, PEER_HEADS),
        in_specs=[pl.BlockSpec((1, 2, 1, PEER_KEYS * 8, LANES), lambda i, h: (h, 0, i, 0, 0))],
        out_specs=[out_spec] * 3,
        out_shape=[out_sds(I32), out_sds(I32), out_sds(F32)],
        compiler_params=pltpu.CompilerParams(
            dimension_semantics=("arbitrary", "arbitrary"),
            vmem_limit_bytes=VMEM_LIMIT_BYTES),
        name="peer_route",
    )(st)
    return [x.reshape(tiles, slots * 8, LANES) for x in outs]


W_PITCH = 136
BUILD_UNROLL = 32
HI16 = 0xFFFF0000


def _peer_kernel(xn_ref, x1_ref, at_ref, bt_ref, gt_ref, wut_ref, wv_ref, gfin_ref,
                 out_ref, wbuf, acc, zs, a_s, b_s, g_s, *, cw):
    tb = xn_ref.shape[0]
    c = pl.program_id(1)
    n_chunks = pl.num_programs(1) - 1
    slots = PEER_HEADS * PEER_TOPK

    @pl.when(c == 0)
    def _build_gates():
        acc[...] = jnp.zeros_like(acc)
        zs[1] = jnp.zeros(zs.shape[1:], zs.dtype)
        g0 = lax.rem(pl.program_id(0), ROUTE_TOKENS // tb) * (tb // LANES)
        for g in range(tb // LANES):
            rows = pl.ds(g * LANES, LANES)
            grp = pl.ds(g0 + g, slots, stride=8)
            a_s[rows, :] = at_ref[0, grp, :].T
            b_s[rows, :] = bt_ref[0, grp, :].T
            g_s[rows, :] = gt_ref[0, grp, :].T
        ids = lax.broadcasted_iota(I32, (PEER_KEYS, slots), 0)

        def gate_bits(t):
            arow = a_s[pl.ds(t, 1), :]
            brow = b_s[pl.ds(t, 1), :]
            grow = g_s[pl.ds(t, 1), :]
            ga = jnp.where(ids == arow, grow, 0.0).astype(BF16)
            ob = jnp.where(ids == brow, 1.0, 0.0).astype(BF16)
            wt = lax.dot_general(ga, ob, _NT, preferred_element_type=F32)
            return lax.bitcast_convert_type(wt, jnp.uint32)

        def token_pairs(i, carry):
            for u in range(BUILD_UNROLL // 2):
                p = i * (BUILD_UNROLL // 2) + u
                packed = (gate_bits(2 * p + 1) & jnp.uint32(HI16)) | (gate_bits(2 * p) >> 16)
                wbuf[pl.ds(pl.multiple_of(p * W_PITCH, 8), PEER_KEYS), :] = packed
            return carry

        lax.fori_loop(0, tb // BUILD_UNROLL, token_pairs, 0)

    wr = lax.rem(c, 2)
    cc = jnp.minimum(c, n_chunks - 1)
    hid = jnp.dot(xn_ref[...], wut_ref[...], preferred_element_type=F32)
    for j in range(cw):
        cols = slice(j * PEER_KEYS, (j + 1) * PEER_KEYS)
        w = wbuf[pl.ds(cc * cw + j, tb // 2, stride=W_PITCH), :]
        gate = pltpu.bitcast(w, BF16)
        zs[wr, :, cols] = jax.nn.gelu(hid[:, cols].astype(BF16)) * gate
    acc[...] += jnp.dot(zs[1 - wr], wv_ref[...], preferred_element_type=F32)

    @pl.when(c == n_chunks)
    def _finish():
        out_ref[...] = _rms(x1_ref[...] + acc[...], gfin_ref[...])


def _peer_call(xn, x1, a_t, b_t, g_t, peer_wu, peer_wv, g_final, tb, cw):
    t, d = xn.shape
    n_exp = peer_wu.shape[0]
    hk = PEER_HEADS * PEER_TOPK
    wut = peer_wu.astype(BF16).T
    wv = peer_wv.astype(BF16)
    n_chunks = PEER_KEYS // cw
    tok = lambda i, c: (i, 0)
    assert ROUTE_TOKENS % tb == 0
    tab = pl.BlockSpec((1, hk * 8, LANES), lambda i, c: (i // (ROUTE_TOKENS // tb), 0, 0))
    return pl.pallas_call(
        functools.partial(_peer_kernel, cw=cw),
        grid=(t // tb, n_chunks + 1),
        in_specs=[
            pl.BlockSpec((tb, d), tok),
            pl.BlockSpec((tb, d), tok),
            tab, tab, tab,
            pl.BlockSpec((d, cw * PEER_KEYS), lambda i, c: (0, jnp.minimum(c, n_chunks - 1))),
            pl.BlockSpec((cw * PEER_KEYS, d), lambda i, c: (jnp.maximum(c - 1, 0), 0)),
            pl.BlockSpec((1, d), lambda i, c: (0, 0)),
        ],
        out_specs=pl.BlockSpec((tb, d), tok),
        out_shape=jax.ShapeDtypeStruct((t, d), F32),
        scratch_shapes=[
            pltpu.VMEM((tb // 2 * W_PITCH, PEER_KEYS), jnp.uint32),
            pltpu.VMEM((tb, d), F32),
            pltpu.VMEM((2, tb, cw * PEER_KEYS), BF16),
            pltpu.VMEM((tb, hk), I32),
            pltpu.VMEM((tb, hk), I32),
            pltpu.VMEM((tb, hk), F32),
        ],
        compiler_params=pltpu.CompilerParams(
            dimension_semantics=("arbitrary", "arbitrary"),
            vmem_limit_bytes=PEER_VMEM_LIMIT_BYTES),
        name="peer_dense",
    )(xn, x1, a_t, b_t, g_t, wut, wv, g_final.reshape(1, d))


def kernel(x, w_in, lam_q1, lam_k1, lam_q2, lam_k2, g_subln, w_s, b_s, g_gv, g_gout,
           w_out, g_mix, g_ffn, peer_wq, peer_keys, peer_wu, peer_wv, g_final):
    batch, seq, d = x.shape
    depth = w_in.shape[0]
    assert depth == 1, "LAM_INIT and the single pass below assume one layer"
    t = batch * seq
    x2 = x.reshape(t, d)
    l = 0
    q, k, v, y = _inproj_call(x2, g_mix[l], w_in[l], w_s[l], b_s[l], g_gv[l], g_gout[l],
                              tm=min(512, t))
    o = _attn_call(q, k, v, lam_q1[l], lam_k1[l], lam_q2[l], lam_k2[l], g_subln[l],
                   batch, seq, qb=min(256, seq))
    x1, xn, st = _outproj_call(o, y, x2, w_out[l], g_ffn[l], peer_wq[l], peer_keys[l],
                               tm=min(512, t))
    a_t, b_t, g_t = _route_call(st)
    out = _peer_call(xn, x1, a_t, b_t, g_t, peer_wu[l], peer_wv[l], g_final,
                     tb=min(512, t), cw=8)
    return out.reshape(batch, seq, d)
```

```python
import functools
import math

import jax
import jax.numpy as jnp
from jax import lax
from jax.experimental import pallas as pl
from jax.experimental.pallas import tpu as pltpu

F32 = jnp.float32
BF16 = jnp.bfloat16
I32 = jnp.int32

EPS = 1e-6
LANES = 128
CHUNK = 64
ATT_HEADS = 4
ATT_HEAD_DIM = 64
ATT_VDIM = 2 * ATT_HEAD_DIM
ATT_WIDTH = ATT_HEADS * ATT_VDIM
GMLP_GROUPS = 4
GMLP_BLOCK = 128
GMLP_GROUP_DIM = 128
GMLP_WIDTH = GMLP_GROUPS * GMLP_GROUP_DIM
PEER_HEADS = 8
PEER_KEYS = 128
KEY_BITS = 7
PEER_TOPK = 16
PEER_HALF = 128
LAM_INIT = 0.8 - 0.6 * math.exp(-0.3 * 0)
ROUTE_TOKENS = 8 * LANES

VMEM_LIMIT_BYTES = 48 * 1024 * 1024
PEER_VMEM_LIMIT_BYTES = 56 * 1024 * 1024

_NT = (((1,), (1,)), ((), ()))


def _rms(x, g):
    return x * lax.rsqrt(jnp.mean(x * x, axis=-1, keepdims=True) + EPS) * g


def _inproj_kernel(x_ref, gmix_ref, win_ref, ws_ref, bsb_ref, ggv_ref, ggout_ref,
                   q_ref, k_ref, v_ref, y_ref):
    tm = x_ref.shape[0]
    h = _rms(x_ref[...], gmix_ref[...]).astype(BF16)

    def proj(lo, hi):
        return jnp.dot(h, win_ref[:, lo:hi], preferred_element_type=F32)

    aw = ATT_WIDTH
    q_ref[...] = (proj(0, aw) * (ATT_HEAD_DIM ** -0.5)).astype(BF16)
    k_ref[...] = proj(aw, 2 * aw).astype(BF16)
    v_ref[...] = proj(2 * aw, 3 * aw).astype(BF16)
    ug = jax.nn.gelu(proj(3 * aw, 3 * aw + GMLP_WIDTH))
    gg = jax.nn.gelu(proj(3 * aw + GMLP_WIDTH, 3 * aw + 2 * GMLP_WIDTH))

    row = lax.broadcasted_iota(I32, (GMLP_BLOCK, GMLP_BLOCK), 0)
    col = lax.broadcasted_iota(I32, (GMLP_BLOCK, GMLP_BLOCK), 1)
    for g in range(GMLP_GROUPS):
        cs = slice(g * GMLP_GROUP_DIM, (g + 1) * GMLP_GROUP_DIM)
        gvn = _rms(gg[:, cs], ggv_ref[:, cs]).astype(BF16)
        w = jnp.where(row >= col, ws_ref[g], 0.0).astype(BF16)
        for blk in range(tm // GMLP_BLOCK):
            rs = slice(blk * GMLP_BLOCK, (blk + 1) * GMLP_BLOCK)
            gate = jnp.dot(w, gvn[rs], preferred_element_type=F32) + bsb_ref[g]
            y_ref[rs, cs] = _rms(ug[rs, cs] * gate, ggout_ref[:, cs]).astype(BF16)


def _inproj_call(x2, g_mix, w_in, w_s, b_s, g_gv, g_gout, tm):
    t, d = x2.shape
    in_w = w_in.shape[1]
    bsb = jnp.broadcast_to(b_s[:, :, None], (GMLP_GROUPS, GMLP_BLOCK, GMLP_GROUP_DIM))
    const2 = lambda i: (0, 0)
    const3 = lambda i: (0, 0, 0)
    tile = lambda i: (i, 0)
    out_sds = jax.ShapeDtypeStruct((t, ATT_WIDTH), BF16)
    return pl.pallas_call(
        _inproj_kernel,
        grid=(t // tm,),
        in_specs=[
            pl.BlockSpec((tm, d), tile),
            pl.BlockSpec((1, d), const2),
            pl.BlockSpec((d, in_w), const2),
            pl.BlockSpec((GMLP_GROUPS, GMLP_BLOCK, GMLP_BLOCK), const3),
            pl.BlockSpec((GMLP_GROUPS, GMLP_BLOCK, GMLP_GROUP_DIM), const3),
            pl.BlockSpec((1, GMLP_WIDTH), const2),
            pl.BlockSpec((1, GMLP_WIDTH), const2),
        ],
        out_specs=[pl.BlockSpec((tm, ATT_WIDTH), tile)] * 4,
        out_shape=[out_sds] * 4,
        compiler_params=pltpu.CompilerParams(
            dimension_semantics=("arbitrary",), vmem_limit_bytes=VMEM_LIMIT_BYTES),
        name="inproj_gmlp",
    )(x2, g_mix.reshape(1, d), w_in.astype(BF16), w_s, bsb,
      g_gv.reshape(1, GMLP_WIDTH), g_gout.reshape(1, GMLP_WIDTH))


def _attn_kernel(slopes_ref, lq1_ref, lk1_ref, lq2_ref, lk2_ref, gsub_ref,
                 q_ref, k_ref, v_ref, o_ref, *, qb):
    seq = q_ref.shape[0]
    slope = slopes_ref[pl.program_id(1)]
    lam = (jnp.exp(jnp.sum(lq1_ref[...] * lk1_ref[...], axis=-1, keepdims=True))
           - jnp.exp(jnp.sum(lq2_ref[...] * lk2_ref[...], axis=-1, keepdims=True))
           + LAM_INIT)
    lane = lax.broadcasted_iota(I32, (qb, ATT_VDIM), 1)
    r = lax.broadcasted_iota(I32, (qb, qb), 0)
    c = lax.broadcasted_iota(I32, (qb, qb), 1)
    diag_bias = jnp.where((c // CHUNK) <= (r // CHUNK),
                          slope * (r - jnp.abs(r - c)).astype(F32), -jnp.inf)
    for qi in range(seq // qb):
        off = qi * qb
        qblk = q_ref[off:off + qb, :]
        kd = k_ref[off:off + qb, :]
        bias_d = diag_bias + slope * off
        if off:
            ko = k_ref[0:off, :]
            bias_o = slope * lax.broadcasted_iota(I32, (1, off), 1).astype(F32)

        def softmax_parts(qm):
            sd = lax.dot_general(qm, kd, _NT, preferred_element_type=F32) + bias_d
            m = jnp.max(sd, axis=-1, keepdims=True)
            if off:
                so = lax.dot_general(qm, ko, _NT, preferred_element_type=F32) + bias_o
                m = jnp.maximum(m, jnp.max(so, axis=-1, keepdims=True))
                po = jnp.exp(so - m)
            pd = jnp.exp(sd - m)
            l = jnp.sum(pd, axis=-1, keepdims=True)
            if off:
                return po, pd, l + jnp.sum(po, axis=-1, keepdims=True)
            return None, pd, l

        po0, pd0, l0 = softmax_parts(jnp.where(lane < ATT_HEAD_DIM, qblk, jnp.zeros_like(qblk)))
        po1, pd1, l1 = softmax_parts(jnp.where(lane >= ATT_HEAD_DIM, qblk, jnp.zeros_like(qblk)))
        w0 = 1.0 / l0
        w1 = lam / l1
        o = jnp.dot((pd0 * w0 - pd1 * w1).astype(BF16), v_ref[off:off + qb, :],
                    preferred_element_type=F32)
        if off:
            o += jnp.dot((po0 * w0 - po1 * w1).astype(BF16), v_ref[0:off, :],
                         preferred_element_type=F32)
        o = _rms(o, gsub_ref[...]) * (1.0 - LAM_INIT)
        o_ref[off:off + qb, :] = o.astype(BF16)


def _attn_call(q, k, v, lam_q1, lam_k1, lam_q2, lam_k2, g_subln, batch, seq, qb):
    t = q.shape[0]
    slopes = 2.0 ** (-8.0 * jnp.arange(1, ATT_HEADS + 1, dtype=F32) / ATT_HEADS)
    blk = pl.BlockSpec((seq, ATT_VDIM), lambda b, h: (b, h))
    vec = lambda n: pl.BlockSpec((1, n), lambda b, h: (0, 0))
    return pl.pallas_call(
        functools.partial(_attn_kernel, qb=qb),
        grid=(batch, ATT_HEADS),
        in_specs=[pl.BlockSpec(memory_space=pltpu.SMEM)]
                 + [vec(ATT_HEAD_DIM)] * 4 + [vec(ATT_VDIM)] + [blk] * 3,
        out_specs=blk,
        out_shape=jax.ShapeDtypeStruct((t, ATT_WIDTH), BF16),
        compiler_params=pltpu.CompilerParams(
            dimension_semantics=("arbitrary", "arbitrary"),
            vmem_limit_bytes=VMEM_LIMIT_BYTES),
        name="diff_attn",
    )(slopes, lam_q1.reshape(1, -1), lam_k1.reshape(1, -1), lam_q2.reshape(1, -1),
      lam_k2.reshape(1, -1), g_subln.reshape(1, -1), q, k, v)


def _outproj_kernel(o_ref, y_ref, x_ref, wo_ref, gffn_ref, wq_ref, keys_ref,
                    x1_ref, xn_ref, st_ref):
    x1 = (x_ref[...]
          + jnp.dot(o_ref[...], wo_ref[0:ATT_WIDTH, :], preferred_element_type=F32)
          + jnp.dot(y_ref[...], wo_ref[ATT_WIDTH:, :], preferred_element_type=F32))
    x1_ref[...] = x1
    xn = _rms(x1, gffn_ref[...]).astype(BF16)
    xn_ref[...] = xn
    qp = jnp.dot(xn, wq_ref[...], preferred_element_type=F32).astype(BF16)
    groups = st_ref.shape[4]
    for hp in range(2 * PEER_HEADS):
        s = lax.dot_general(
            keys_ref[hp], qp[:, hp * PEER_HALF:(hp + 1) * PEER_HALF], _NT,
            preferred_element_type=F32)
        for g in range(groups):
            st_ref[hp // 2, hp % 2, 0, :, g] = s[:, g * LANES:(g + 1) * LANES].reshape(
                PEER_KEYS // 8, 8, LANES)


def _outproj_call(o, y, x2, w_out, g_ffn, peer_wq, peer_keys, tm):
    t, d = x2.shape
    nq = peer_wq.shape[1]
    tile = lambda i: (i, 0)
    const2 = lambda i: (0, 0)
    keys = peer_keys.reshape(2 * PEER_HEADS, PEER_KEYS, PEER_HALF).astype(BF16)
    assert ROUTE_TOKENS % tm == 0 and t % ROUTE_TOKENS == 0
    steps_per_tile = ROUTE_TOKENS // tm
    return pl.pallas_call(
        _outproj_kernel,
        grid=(t // tm,),
        in_specs=[
            pl.BlockSpec((tm, ATT_WIDTH), tile),
            pl.BlockSpec((tm, GMLP_WIDTH), tile),
            pl.BlockSpec((tm, d), tile),
            pl.BlockSpec((ATT_WIDTH + GMLP_WIDTH, d), const2),
            pl.BlockSpec((1, d), const2),
            pl.BlockSpec((d, nq), const2),
            pl.BlockSpec((2 * PEER_HEADS, PEER_KEYS, PEER_HALF), lambda i: (0, 0, 0)),
        ],
        out_specs=[
            pl.BlockSpec((tm, d), tile),
            pl.BlockSpec((tm, d), tile),
            pl.BlockSpec((PEER_HEADS, 2, 1, PEER_KEYS // 8, tm // LANES, 8, LANES),
                         lambda i: (0, 0, i // steps_per_tile, 0, i % steps_per_tile, 0, 0)),
        ],
        out_shape=[
            jax.ShapeDtypeStruct((t, d), F32),
            jax.ShapeDtypeStruct((t, d), BF16),
            jax.ShapeDtypeStruct((PEER_HEADS, 2, t // ROUTE_TOKENS, PEER_KEYS // 8, 8, 8, LANES), F32),
        ],
        compiler_params=pltpu.CompilerParams(
            dimension_semantics=("arbitrary",), vmem_limit_bytes=VMEM_LIMIT_BYTES),
        name="outproj_scores",
    )(o, y, x2, w_out.astype(BF16), g_ffn.reshape(1, d), peer_wq.astype(BF16), keys)


def _extract_topk(s, ids, k):
    big = jnp.iinfo(jnp.int32).max
    vals, picks = [], []
    for _ in range(k):
        m = jnp.max(s, axis=0, keepdims=True)
        pick = jnp.min(jnp.where(s == m, ids, big), axis=0, keepdims=True)
        vals.append(m)
        picks.append(pick)
        s = jnp.where(ids == pick, -jnp.inf, s)
    return jnp.concatenate(vals, axis=0), jnp.concatenate(picks, axis=0)


def _take_rows(table, sel, n):
    out = jnp.zeros(sel.shape, table.dtype)
    for j in range(n):
        out = jnp.where(sel == j, table[j:j + 1], out)
    return out


def _sort_network(n):
    pairs = []

    def merge(lo, length, r):
        step = 2 * r
        if step < length:
            merge(lo, length, step)
            merge(lo + r, length, step)
            pairs.extend((i, i + r) for i in range(lo + r, lo + length - r, step))
        else:
            pairs.append((lo, lo + r))

    def sort(lo, length):
        if length > 1:
            sort(lo, length // 2)
            sort(lo + length // 2, length // 2)
            merge(lo, length, 1)

    sort(0, n)
    return pairs


def _compare_exchange(vals, ids, i, j):
    a, b = vals[i], vals[j]
    ge = a >= b
    vals[i], vals[j] = jnp.maximum(a, b), jnp.minimum(a, b)
    ids[i], ids[j] = jnp.where(ge, ids[i], ids[j]), jnp.where(ge, ids[j], ids[i])


def _bitonic_merge(vals, ids):
    n = len(vals)
    d = n // 2
    while d:
        for i in range(n):
            if (i // d) % 2 == 0:
                _compare_exchange(vals, ids, i, i + d)
        d //= 2


def _merge_top(av, ai, bv, bi):
    n = len(av)
    cv, ci = [], []
    for r in range(n):
        a, b = av[r], bv[n - 1 - r]
        cv.append(jnp.maximum(a, b))
        ci.append(jnp.where(a >= b, ai[r], bi[n - 1 - r]))
    _bitonic_merge(cv, ci)
    return cv, ci


def _merge_all(av, ai, bv, bi):
    cv, ci = av + bv[::-1], ai + bi[::-1]
    _bitonic_merge(cv, ci)
    return cv, ci


def _tie_flag(rows, top, k):
    flag = jnp.zeros(top[0].shape, I32)
    for r in range(k - 1):
        flag = jnp.where(top[r] == top[r + 1], 1, flag)
    count = jnp.zeros(top[0].shape, I32)
    for row in rows:
        count = count + jnp.where(row >= top[k - 1], 1, 0)
    return jnp.where(count != k, 1, flag)


def _net_topk(rows, k):
    pairs = _sort_network(k)
    lists = []
    for g in range(0, len(rows), k):
        vals = list(rows[g:g + k])
        ids = [jnp.full(rows[0].shape, g + j, I32) for j in range(k)]
        for i, j in pairs:
            _compare_exchange(vals, ids, i, j)
        lists.append((vals, ids))
    while len(lists) > 1:
        lists = [_merge_top(*lists[m], *lists[m + 1]) for m in range(0, len(lists), 2)]
    vals, ids = lists[0]
    return vals, ids, _tie_flag(rows, vals, k)


def _route_kernel(st_ref, a_ref, b_ref, g_ref):
    k = PEER_TOPK
    hk = k // 2
    shape = (8, LANES)

    def key_rows(side):
        return [st_ref[0, side, 0, pl.ds((n // 8) * 64 + n % 8, 8, stride=8), :]
                for n in range(PEER_KEYS)]
    widths = [k] + [hk] * (hk - 1)

    def finish(best, expert):
        e = jnp.exp(best - best[0:1])
        a_ref[0] = expert >> KEY_BITS
        b_ref[0] = expert & (PEER_KEYS - 1)
        g_ref[0] = e / jnp.sum(e, axis=0, keepdims=True)

    v1, i1, f1 = _net_topk(key_rows(0), k)
    v2, i2, f2 = _net_topk(key_rows(1), k)
    base = [i * PEER_KEYS for i in i1]
    groups = [([v1[k1] + v2[k2] for k2 in range(w)], [base[k1] + i2[k2] for k2 in range(w)])
              for k1, w in enumerate(widths)]
    groups.append(([v1[k1] + v2[0] for k1 in range(hk, k)], [base[k1] + i2[0] for k1 in range(hk, k)]))
    cand_rows = [v for g in groups for v in g[0]]
    full = [groups[0]] + [_merge_all(*groups[m], *groups[m + 1]) for m in range(1, len(groups), 2)]
    while len(full) > 1:
        nxt = [_merge_top(*full[m], *full[m + 1]) for m in range(0, len(full) - 1, 2)]
        full = nxt + full[len(full) - len(full) % 2:]
    best, expert = full[0]
    finish(jnp.stack(best), jnp.stack(expert))
    tie = jnp.maximum(jnp.maximum(f1, f2), _tie_flag(cand_rows, best, k))

    @pl.when(jnp.max(tie) > 0)
    def _exact():
        key_ids = lax.broadcasted_iota(I32, (PEER_KEYS,) + shape, 0)
        flat = jnp.concatenate(
            [k1 * k + lax.broadcasted_iota(I32, (w,) + shape, 0) for k1, w in enumerate(widths)]
            + [(hk + lax.broadcasted_iota(I32, (hk,) + shape, 0)) * k], axis=0)
        xv1, xi1 = _extract_topk(jnp.stack(key_rows(0)), key_ids, k)
        xv2, xi2 = _extract_topk(jnp.stack(key_rows(1)), key_ids, k)
        cand = jnp.concatenate(
            [xv1[k1:k1 + 1] + xv2[0:w] for k1, w in enumerate(widths)]
            + [xv1[hk:k] + xv2[0:1]], axis=0)
        xbest, pick = _extract_topk(cand, flat, k)
        sel1 = pick // k
        sel2 = pick - sel1 * k
        finish(xbest, _take_rows(xi1, sel1, k) * PEER_KEYS + _take_rows(xi2, sel2, k))


def _route_call(st):
    tiles = st.shape[2]
    slots = PEER_HEADS * PEER_TOPK
    st = st.reshape(PEER_HEADS, 2, tiles, PEER_KEYS * 8, LANES)
    out_spec = pl.BlockSpec((1, PEER_TOPK, 8, LANES), lambda i, h: (i, h, 0, 0))
    out_sds = lambda dt: jax.ShapeDtypeStruct((tiles, slots, 8, LANES), dt)
    outs = pl.pallas_call(
        _route_kernel,
        grid=(tiles, PEER_HEADS),
        in_specs=[pl.BlockSpec((1, 2, 1, PEER_KEYS * 8, LANES), lambda i, h: (h, 0, i, 0, 0))],
        out_specs=[out_spec] * 3,
        out_shape=[out_sds(I32), out_sds(I32), out_sds(F32)],
        compiler_params=pltpu.CompilerParams(
            dimension_semantics=("arbitrary", "arbitrary"),
            vmem_limit_bytes=VMEM_LIMIT_BYTES),
        name="peer_route",
    )(st)
    return [x.reshape(tiles, slots * 8, LANES) for x in outs]


W_PITCH = 136
BUILD_UNROLL = 64
HI16 = 0xFFFF0000
GELU_K0 = math.sqrt(2.0 / math.pi)
GELU_K1 = 0.044715 * GELU_K0


def _peer_kernel(xn_ref, x1_ref, at_ref, bt_ref, gt_ref, wut_ref, wv_ref, gfin_ref,
                 out_ref, wbuf, acc, hb, zs, a_s, b_s, g_s, *, cw):
    tb = xn_ref.shape[0]
    c = pl.program_id(1)
    n_chunks = pl.num_programs(1) - 2
    slots = PEER_HEADS * PEER_TOPK

    @pl.when(c == 0)
    def _build_gates():
        acc[...] = jnp.zeros_like(acc)
        zs[...] = jnp.zeros_like(zs)
        hb[1] = jnp.zeros(hb.shape[1:], hb.dtype)
        g0 = lax.rem(pl.program_id(0), ROUTE_TOKENS // tb) * (tb // LANES)
        for g in range(tb // LANES):
            rows = pl.ds(g * LANES, LANES)
            grp = pl.ds(g0 + g, slots, stride=8)
            a_s[rows, :] = at_ref[0, grp, :].T
            b_s[rows, :] = bt_ref[0, grp, :].T
            g_s[rows, :] = gt_ref[0, grp, :].T
        ids = lax.broadcasted_iota(I32, (PEER_KEYS, slots), 0)

        def gate_bits(t):
            arow = a_s[pl.ds(t, 1), :]
            brow = b_s[pl.ds(t, 1), :]
            grow = g_s[pl.ds(t, 1), :]
            ga = jnp.where(ids == arow, 0.5 * grow, 0.0).astype(BF16)
            ob = jnp.where(ids == brow, 1.0, 0.0).astype(BF16)
            wt = lax.dot_general(ga, ob, _NT, preferred_element_type=F32)
            return lax.bitcast_convert_type(wt, jnp.uint32)

        def token_pairs(i, carry):
            for u in range(BUILD_UNROLL // 2):
                p = i * (BUILD_UNROLL // 2) + u
                packed = (gate_bits(2 * p + 1) & jnp.uint32(HI16)) | (gate_bits(2 * p) >> 16)
                wbuf[pl.ds(pl.multiple_of(p * W_PITCH, 8), PEER_KEYS), :] = packed
            return carry

        lax.fori_loop(0, tb // BUILD_UNROLL, token_pairs, 0)

    cur = lax.rem(c, 2)
    prev = 1 - cur
    cb = jnp.clip(c - 1, 0, n_chunks - 1)
    for j in range(cw):
        cols = slice(j * PEER_KEYS, (j + 1) * PEER_KEYS)
        w = wbuf[pl.ds(cb * cw + j, tb // 2, stride=W_PITCH), :]
        half_gate = pltpu.bitcast(w, BF16)
        x = hb[prev, :, cols]
        u = x * (x * x * GELU_K1 + GELU_K0)
        zs[cur, :, cols] = (x * jnp.tanh(u) + x) * half_gate
    acc[...] += jnp.dot(zs[prev], wv_ref[...], preferred_element_type=F32)
    hb[cur] = jnp.dot(xn_ref[...], wut_ref[...], preferred_element_type=F32).astype(BF16)

    @pl.when(c == n_chunks + 1)
    def _finish():
        out_ref[...] = _rms(x1_ref[...] + acc[...], gfin_ref[...])


def _peer_call(xn, x1, a_t, b_t, g_t, peer_wu, peer_wv, g_final, tb, cw):
    t, d = xn.shape
    n_exp = peer_wu.shape[0]
    hk = PEER_HEADS * PEER_TOPK
    wut = peer_wu.astype(BF16).T
    wv = peer_wv.astype(BF16)
    n_chunks = PEER_KEYS // cw
    tok = lambda i, c: (i, 0)
    assert ROUTE_TOKENS % tb == 0
    tab = pl.BlockSpec((1, hk * 8, LANES), lambda i, c: (i // (ROUTE_TOKENS // tb), 0, 0))
    return pl.pallas_call(
        functools.partial(_peer_kernel, cw=cw),
        grid=(t // tb, n_chunks + 2),
        in_specs=[
            pl.BlockSpec((tb, d), tok),
            pl.BlockSpec((tb, d), tok),
            tab, tab, tab,
            pl.BlockSpec((d, cw * PEER_KEYS), lambda i, c: (0, jnp.minimum(c, n_chunks - 1))),
            pl.BlockSpec((cw * PEER_KEYS, d), lambda i, c: (jnp.clip(c - 2, 0, n_chunks - 1), 0)),
            pl.BlockSpec((1, d), lambda i, c: (0, 0)),
        ],
        out_specs=pl.BlockSpec((tb, d), tok),
        out_shape=jax.ShapeDtypeStruct((t, d), F32),
        scratch_shapes=[
            pltpu.VMEM((tb // 2 * W_PITCH, PEER_KEYS), jnp.uint32),
            pltpu.VMEM((tb, d), F32),
            pltpu.VMEM((2, tb, cw * PEER_KEYS), BF16),
            pltpu.VMEM((2, tb, cw * PEER_KEYS), BF16),
            pltpu.VMEM((tb, hk), I32),
            pltpu.VMEM((tb, hk), I32),
            pltpu.VMEM((tb, hk), F32),
        ],
        compiler_params=pltpu.CompilerParams(
            dimension_semantics=("arbitrary", "arbitrary"),
            vmem_limit_bytes=PEER_VMEM_LIMIT_BYTES),
        name="peer_dense",
    )(xn, x1, a_t, b_t, g_t, wut, wv, g_final.reshape(1, d))


def kernel(x, w_in, lam_q1, lam_k1, lam_q2, lam_k2, g_subln, w_s, b_s, g_gv, g_gout,
           w_out, g_mix, g_ffn, peer_wq, peer_keys, peer_wu, peer_wv, g_final):
    batch, seq, d = x.shape
    depth = w_in.shape[0]
    assert depth == 1, "LAM_INIT and the single pass below assume one layer"
    t = batch * seq
    x2 = x.reshape(t, d)
    l = 0
    q, k, v, y = _inproj_call(x2, g_mix[l], w_in[l], w_s[l], b_s[l], g_gv[l], g_gout[l],
                              tm=min(512, t))
    o = _attn_call(q, k, v, lam_q1[l], lam_k1[l], lam_q2[l], lam_k2[l], g_subln[l],
                   batch, seq, qb=min(256, seq))
    x1, xn, st = _outproj_call(o, y, x2, w_out[l], g_ffn[l], peer_wq[l], peer_keys[l],
                               tm=min(512, t))
    a_t, b_t, g_t = _route_call(st)
    out = _peer_call(xn, x1, a_t, b_t, g_t, peer_wu[l], peer_wv[l], g_final,
                     tb=min(512, t), cw=8)
    return out.reshape(batch, seq, d)
```

```python
import functools
import math

import jax
import jax.numpy as jnp
from jax import lax
from jax.experimental import pallas as pl
from jax.experimental.pallas import tpu as pltpu

F32 = jnp.float32
BF16 = jnp.bfloat16
I32 = jnp.int32

EPS = 1e-6
LANES = 128
CHUNK = 64
ATT_HEADS = 4
ATT_HEAD_DIM = 64
ATT_VDIM = 2 * ATT_HEAD_DIM
ATT_WIDTH = ATT_HEADS * ATT_VDIM
GMLP_GROUPS = 4
GMLP_BLOCK = 128
GMLP_GROUP_DIM = 128
GMLP_WIDTH = GMLP_GROUPS * GMLP_GROUP_DIM
PEER_HEADS = 8
PEER_KEYS = 128
KEY_BITS = 7
PEER_TOPK = 16
PEER_HALF = 128
LAM_INIT = 0.8 - 0.6 * math.exp(-0.3 * 0)
ROUTE_TOKENS = 8 * LANES

VMEM_LIMIT_BYTES = 48 * 1024 * 1024
PEER_VMEM_LIMIT_BYTES = 56 * 1024 * 1024

_NT = (((1,), (1,)), ((), ()))


def _rms(x, g):
    return x * lax.rsqrt(jnp.mean(x * x, axis=-1, keepdims=True) + EPS) * g


def _inproj_kernel(x_ref, gmix_ref, win_ref, ws_ref, bsb_ref, ggv_ref, ggout_ref,
                   q_ref, k_ref, v_ref, y_ref):
    tm = x_ref.shape[0]
    h = _rms(x_ref[...], gmix_ref[...]).astype(BF16)

    def proj(lo, hi):
        return jnp.dot(h, win_ref[:, lo:hi], preferred_element_type=F32)

    aw = ATT_WIDTH
    q_ref[...] = (proj(0, aw) * (ATT_HEAD_DIM ** -0.5)).astype(BF16)
    k_ref[...] = proj(aw, 2 * aw).astype(BF16)
    v_ref[...] = proj(2 * aw, 3 * aw).astype(BF16)
    ug = jax.nn.gelu(proj(3 * aw, 3 * aw + GMLP_WIDTH))
    gg = jax.nn.gelu(proj(3 * aw + GMLP_WIDTH, 3 * aw + 2 * GMLP_WIDTH))

    row = lax.broadcasted_iota(I32, (GMLP_BLOCK, GMLP_BLOCK), 0)
    col = lax.broadcasted_iota(I32, (GMLP_BLOCK, GMLP_BLOCK), 1)
    for g in range(GMLP_GROUPS):
        cs = slice(g * GMLP_GROUP_DIM, (g + 1) * GMLP_GROUP_DIM)
        gvn = _rms(gg[:, cs], ggv_ref[:, cs]).astype(BF16)
        w = jnp.where(row >= col, ws_ref[g], 0.0).astype(BF16)
        for blk in range(tm // GMLP_BLOCK):
            rs = slice(blk * GMLP_BLOCK, (blk + 1) * GMLP_BLOCK)
            gate = jnp.dot(w, gvn[rs], preferred_element_type=F32) + bsb_ref[g]
            y_ref[rs, cs] = _rms(ug[rs, cs] * gate, ggout_ref[:, cs]).astype(BF16)


def _inproj_call(x2, g_mix, w_in, w_s, b_s, g_gv, g_gout, tm):
    t, d = x2.shape
    in_w = w_in.shape[1]
    bsb = jnp.broadcast_to(b_s[:, :, None], (GMLP_GROUPS, GMLP_BLOCK, GMLP_GROUP_DIM))
    const2 = lambda i: (0, 0)
    const3 = lambda i: (0, 0, 0)
    tile = lambda i: (i, 0)
    out_sds = jax.ShapeDtypeStruct((t, ATT_WIDTH), BF16)
    return pl.pallas_call(
        _inproj_kernel,
        grid=(t // tm,),
        in_specs=[
            pl.BlockSpec((tm, d), tile),
            pl.BlockSpec((1, d), const2),
            pl.BlockSpec((d, in_w), const2),
            pl.BlockSpec((GMLP_GROUPS, GMLP_BLOCK, GMLP_BLOCK), const3),
            pl.BlockSpec((GMLP_GROUPS, GMLP_BLOCK, GMLP_GROUP_DIM), const3),
            pl.BlockSpec((1, GMLP_WIDTH), const2),
            pl.BlockSpec((1, GMLP_WIDTH), const2),
        ],
        out_specs=[pl.BlockSpec((tm, ATT_WIDTH), tile)] * 4,
        out_shape=[out_sds] * 4,
        compiler_params=pltpu.CompilerParams(
            dimension_semantics=("arbitrary",), vmem_limit_bytes=VMEM_LIMIT_BYTES),
        name="inproj_gmlp",
    )(x2, g_mix.reshape(1, d), w_in.astype(BF16), w_s, bsb,
      g_gv.reshape(1, GMLP_WIDTH), g_gout.reshape(1, GMLP_WIDTH))


def _attn_kernel(slopes_ref, lq1_ref, lk1_ref, lq2_ref, lk2_ref, gsub_ref,
                 q_ref, k_ref, v_ref, o_ref, *, qb):
    seq = q_ref.shape[0]
    slope = slopes_ref[pl.program_id(1)]
    lam = (jnp.exp(jnp.sum(lq1_ref[...] * lk1_ref[...], axis=-1, keepdims=True))
           - jnp.exp(jnp.sum(lq2_ref[...] * lk2_ref[...], axis=-1, keepdims=True))
           + LAM_INIT)
    lane = lax.broadcasted_iota(I32, (qb, ATT_VDIM), 1)
    r = lax.broadcasted_iota(I32, (qb, qb), 0)
    c = lax.broadcasted_iota(I32, (qb, qb), 1)
    diag_bias = jnp.where((c // CHUNK) <= (r // CHUNK),
                          slope * (r - jnp.abs(r - c)).astype(F32), -jnp.inf)
    for qi in range(seq // qb):
        off = qi * qb
        qblk = q_ref[off:off + qb, :]
        kd = k_ref[off:off + qb, :]
        bias_d = diag_bias + slope * off
        if off:
            ko = k_ref[0:off, :]
            bias_o = slope * lax.broadcasted_iota(I32, (1, off), 1).astype(F32)

        def softmax_parts(qm):
            sd = lax.dot_general(qm, kd, _NT, preferred_element_type=F32) + bias_d
            m = jnp.max(sd, axis=-1, keepdims=True)
            if off:
                so = lax.dot_general(qm, ko, _NT, preferred_element_type=F32) + bias_o
                m = jnp.maximum(m, jnp.max(so, axis=-1, keepdims=True))
                po = jnp.exp(so - m)
            pd = jnp.exp(sd - m)
            l = jnp.sum(pd, axis=-1, keepdims=True)
            if off:
                return po, pd, l + jnp.sum(po, axis=-1, keepdims=True)
            return None, pd, l

        po0, pd0, l0 = softmax_parts(jnp.where(lane < ATT_HEAD_DIM, qblk, jnp.zeros_like(qblk)))
        po1, pd1, l1 = softmax_parts(jnp.where(lane >= ATT_HEAD_DIM, qblk, jnp.zeros_like(qblk)))
        w0 = 1.0 / l0
        w1 = lam / l1
        o = jnp.dot((pd0 * w0 - pd1 * w1).astype(BF16), v_ref[off:off + qb, :],
                    preferred_element_type=F32)
        if off:
            o += jnp.dot((po0 * w0 - po1 * w1).astype(BF16), v_ref[0:off, :],
                         preferred_element_type=F32)
        o = _rms(o, gsub_ref[...]) * (1.0 - LAM_INIT)
        o_ref[off:off + qb, :] = o.astype(BF16)


def _attn_call(q, k, v, lam_q1, lam_k1, lam_q2, lam_k2, g_subln, batch, seq, qb):
    t = q.shape[0]
    slopes = 2.0 ** (-8.0 * jnp.arange(1, ATT_HEADS + 1, dtype=F32) / ATT_HEADS)
    blk = pl.BlockSpec((seq, ATT_VDIM), lambda b, h: (b, h))
    vec = lambda n: pl.BlockSpec((1, n), lambda b, h: (0, 0))
    return pl.pallas_call(
        functools.partial(_attn_kernel, qb=qb),
        grid=(batch, ATT_HEADS),
        in_specs=[pl.BlockSpec(memory_space=pltpu.SMEM)]
                 + [vec(ATT_HEAD_DIM)] * 4 + [vec(ATT_VDIM)] + [blk] * 3,
        out_specs=blk,
        out_shape=jax.ShapeDtypeStruct((t, ATT_WIDTH), BF16),
        compiler_params=pltpu.CompilerParams(
            dimension_semantics=("arbitrary", "arbitrary"),
            vmem_limit_bytes=VMEM_LIMIT_BYTES),
        name="diff_attn",
    )(slopes, lam_q1.reshape(1, -1), lam_k1.reshape(1, -1), lam_q2.reshape(1, -1),
      lam_k2.reshape(1, -1), g_subln.reshape(1, -1), q, k, v)


def _outproj_kernel(o_ref, y_ref, x_ref, wo_ref, gffn_ref, wq_ref, keys_ref,
                    x1_ref, xn_ref, st_ref):
    x1 = (x_ref[...]
          + jnp.dot(o_ref[...], wo_ref[0:ATT_WIDTH, :], preferred_element_type=F32)
          + jnp.dot(y_ref[...], wo_ref[ATT_WIDTH:, :], preferred_element_type=F32))
    x1_ref[...] = x1
    xn = _rms(x1, gffn_ref[...]).astype(BF16)
    xn_ref[...] = xn
    qp = jnp.dot(xn, wq_ref[...], preferred_element_type=F32).astype(BF16)
    groups = st_ref.shape[4]
    for hp in range(2 * PEER_HEADS):
        s = lax.dot_general(
            keys_ref[hp], qp[:, hp * PEER_HALF:(hp + 1) * PEER_HALF], _NT,
            preferred_element_type=F32)
        for g in range(groups):
            st_ref[hp // 2, hp % 2, 0, :, g] = s[:, g * LANES:(g + 1) * LANES].reshape(
                PEER_KEYS // 8, 8, LANES)


def _outproj_call(o, y, x2, w_out, g_ffn, peer_wq, peer_keys, tm):
    t, d = x2.shape
    nq = peer_wq.shape[1]
    tile = lambda i: (i, 0)
    const2 = lambda i: (0, 0)
    keys = peer_keys.reshape(2 * PEER_HEADS, PEER_KEYS, PEER_HALF).astype(BF16)
    assert ROUTE_TOKENS % tm == 0 and t % ROUTE_TOKENS == 0
    steps_per_tile = ROUTE_TOKENS // tm
    return pl.pallas_call(
        _outproj_kernel,
        grid=(t // tm,),
        in_specs=[
            pl.BlockSpec((tm, ATT_WIDTH), tile),
            pl.BlockSpec((tm, GMLP_WIDTH), tile),
            pl.BlockSpec((tm, d), tile),
            pl.BlockSpec((ATT_WIDTH + GMLP_WIDTH, d), const2),
            pl.BlockSpec((1, d), const2),
            pl.BlockSpec((d, nq), const2),
            pl.BlockSpec((2 * PEER_HEADS, PEER_KEYS, PEER_HALF), lambda i: (0, 0, 0)),
        ],
        out_specs=[
            pl.BlockSpec((tm, d), tile),
            pl.BlockSpec((tm, d), tile),
            pl.BlockSpec((PEER_HEADS, 2, 1, PEER_KEYS // 8, tm // LANES, 8, LANES),
                         lambda i: (0, 0, i // steps_per_tile, 0, i % steps_per_tile, 0, 0)),
        ],
        out_shape=[
            jax.ShapeDtypeStruct((t, d), F32),
            jax.ShapeDtypeStruct((t, d), BF16),
            jax.ShapeDtypeStruct((PEER_HEADS, 2, t // ROUTE_TOKENS, PEER_KEYS // 8, 8, 8, LANES), F32),
        ],
        compiler_params=pltpu.CompilerParams(
            dimension_semantics=("arbitrary",), vmem_limit_bytes=VMEM_LIMIT_BYTES),
        name="outproj_scores",
    )(o, y, x2, w_out.astype(BF16), g_ffn.reshape(1, d), peer_wq.astype(BF16), keys)


def _extract_topk(s, ids, k):
    big = jnp.iinfo(jnp.int32).max
    vals, picks = [], []
    for _ in range(k):
        m = jnp.max(s, axis=0, keepdims=True)
        pick = jnp.min(jnp.where(s == m, ids, big), axis=0, keepdims=True)
        vals.append(m)
        picks.append(pick)
        s = jnp.where(ids == pick, -jnp.inf, s)
    return jnp.concatenate(vals, axis=0), jnp.concatenate(picks, axis=0)


def _take_rows(table, sel, n):
    out = jnp.zeros(sel.shape, table.dtype)
    for j in range(n):
        out = jnp.where(sel == j, table[j:j + 1], out)
    return out


def _sort_network(n):
    pairs = []

    def merge(lo, length, r):
        step = 2 * r
        if step < length:
            merge(lo, length, step)
            merge(lo + r, length, step)
            pairs.extend((i, i + r) for i in range(lo + r, lo + length - r, step))
        else:
            pairs.append((lo, lo + r))

    def sort(lo, length):
        if length > 1:
            sort(lo, length // 2)
            sort(lo + length // 2, length // 2)
            merge(lo, length, 1)

    sort(0, n)
    return pairs


def _compare_exchange(vals, ids, i, j):
    a, b = vals[i], vals[j]
    ge = a >= b
    vals[i], vals[j] = jnp.maximum(a, b), jnp.minimum(a, b)
    ids[i], ids[j] = jnp.where(ge, ids[i], ids[j]), jnp.where(ge, ids[j], ids[i])


def _bitonic_merge(vals, ids):
    n = len(vals)
    d = n // 2
    while d:
        for i in range(n):
            if (i // d) % 2 == 0:
                _compare_exchange(vals, ids, i, i + d)
        d //= 2


def _merge_top(av, ai, bv, bi):
    n = len(av)
    cv, ci = [], []
    for r in range(n):
        a, b = av[r], bv[n - 1 - r]
        cv.append(jnp.maximum(a, b))
        ci.append(jnp.where(a >= b, ai[r], bi[n - 1 - r]))
    _bitonic_merge(cv, ci)
    return cv, ci


def _merge_all(av, ai, bv, bi):
    cv, ci = av + bv[::-1], ai + bi[::-1]
    _bitonic_merge(cv, ci)
    return cv, ci


def _tie_flag(rows, top, k):
    flag = jnp.zeros(top[0].shape, I32)
    for r in range(k - 1):
        flag = jnp.where(top[r] == top[r + 1], 1, flag)
    count = jnp.zeros(top[0].shape, I32)
    for row in rows:
        count = count + jnp.where(row >= top[k - 1], 1, 0)
    return jnp.where(count != k, 1, flag)


def _net_topk(rows, k):
    pairs = _sort_network(k)
    lists = []
    for g in range(0, len(rows), k):
        vals = list(rows[g:g + k])
        ids = [jnp.full(rows[0].shape, g + j, I32) for j in range(k)]
        for i, j in pairs:
            _compare_exchange(vals, ids, i, j)
        lists.append((vals, ids))
    while len(lists) > 1:
        lists = [_merge_top(*lists[m], *lists[m + 1]) for m in range(0, len(lists), 2)]
    vals, ids = lists[0]
    return vals, ids, _tie_flag(rows, vals, k)


def _route_kernel(st_ref, a_ref, b_ref, g_ref):
    k = PEER_TOPK
    hk = k // 2
    shape = (8, LANES)

    def key_rows(side):
        return [st_ref[0, side, 0, pl.ds((n // 8) * 64 + n % 8, 8, stride=8), :]
                for n in range(PEER_KEYS)]
    widths = [k] + [hk] * (hk - 1)

    def finish(best, expert):
        e = jnp.exp(best - best[0:1])
        a_ref[0] = expert >> KEY_BITS
        b_ref[0] = expert & (PEER_KEYS - 1)
        g_ref[0] = e / jnp.sum(e, axis=0, keepdims=True)

    v1, i1, f1 = _net_topk(key_rows(0), k)
    v2, i2, f2 = _net_topk(key_rows(1), k)
    base = [i * PEER_KEYS for i in i1]
    groups = [([v1[k1] + v2[k2] for k2 in range(w)], [base[k1] + i2[k2] for k2 in range(w)])
              for k1, w in enumerate(widths)]
    groups.append(([v1[k1] + v2[0] for k1 in range(hk, k)], [base[k1] + i2[0] for k1 in range(hk, k)]))
    cand_rows = [v for g in groups for v in g[0]]
    full = [groups[0]] + [_merge_all(*groups[m], *groups[m + 1]) for m in range(1, len(groups), 2)]
    while len(full) > 1:
        nxt = [_merge_top(*full[m], *full[m + 1]) for m in range(0, len(full) - 1, 2)]
        full = nxt + full[len(full) - len(full) % 2:]
    best, expert = full[0]
    finish(jnp.stack(best), jnp.stack(expert))
    tie = jnp.maximum(jnp.maximum(f1, f2), _tie_flag(cand_rows, best, k))

    @pl.when(jnp.max(tie) > 0)
    def _exact():
        key_ids = lax.broadcasted_iota(I32, (PEER_KEYS,) + shape, 0)
        flat = jnp.concatenate(
            [k1 * k + lax.broadcasted_iota(I32, (w,) + shape, 0) for k1, w in enumerate(widths)]
            + [(hk + lax.broadcasted_iota(I32, (hk,) + shape, 0)) * k], axis=0)
        xv1, xi1 = _extract_topk(jnp.stack(key_rows(0)), key_ids, k)
        xv2, xi2 = _extract_topk(jnp.stack(key_rows(1)), key_ids, k)
        cand = jnp.concatenate(
            [xv1[k1:k1 + 1] + xv2[0:w] for k1, w in enumerate(widths)]
            + [xv1[hk:k] + xv2[0:1]], axis=0)
        xbest, pick = _extract_topk(cand, flat, k)
        sel1 = pick // k
        sel2 = pick - sel1 * k
        finish(xbest, _take_rows(xi1, sel1, k) * PEER_KEYS + _take_rows(xi2, sel2, k))


def _route_call(st):
    tiles = st.shape[2]
    slots = PEER_HEADS * PEER_TOPK
    st = st.reshape(PEER_HEADS, 2, tiles, PEER_KEYS * 8, LANES)
    out_spec = pl.BlockSpec((1, PEER_TOPK, 8, LANES), lambda i, h: (i, h, 0, 0))
    out_sds = lambda dt: jax.ShapeDtypeStruct((tiles, slots, 8, LANES), dt)
    outs = pl.pallas_call(
        _route_kernel,
        grid=(tiles, PEER_HEADS),
        in_specs=[pl.BlockSpec((1, 2, 1, PEER_KEYS * 8, LANES), lambda i, h: (h, 0, i, 0, 0))],
        out_specs=[out_spec] * 3,
        out_shape=[out_sds(I32), out_sds(I32), out_sds(F32)],
        compiler_params=pltpu.CompilerParams(
            dimension_semantics=("arbitrary", "arbitrary"),
            vmem_limit_bytes=VMEM_LIMIT_BYTES),
        name="peer_route",
    )(st)
    return [x.reshape(tiles, slots * 8, LANES) for x in outs]


W_PITCH = 136
BUILD_UNROLL = 64
HI16 = 0xFFFF0000
GELU_K0 = math.sqrt(2.0 / math.pi)
GELU_K1 = 0.044715 * GELU_K0


def _peer_kernel(xn_ref, x1_ref, at_ref, bt_ref, gt_ref, wut_ref, wv_ref, gfin_ref,
                 out_ref, wbuf, acc, hb, zs, a_s, b_s, g_s, *, cw):
    tb = xn_ref.shape[0]
    c = pl.program_id(1)
    n_chunks = pl.num_programs(1) - 1
    slots = PEER_HEADS * PEER_TOPK

    @pl.when(c == 0)
    def _build_gates():
        acc[...] = jnp.zeros_like(acc)
        hb[1] = jnp.zeros(hb.shape[1:], hb.dtype)
        g0 = lax.rem(pl.program_id(0), ROUTE_TOKENS // tb) * (tb // LANES)
        for g in range(tb // LANES):
            rows = pl.ds(g * LANES, LANES)
            grp = pl.ds(g0 + g, slots, stride=8)
            a_s[rows, :] = at_ref[0, grp, :].T
            b_s[rows, :] = bt_ref[0, grp, :].T
            g_s[rows, :] = gt_ref[0, grp, :].T
        ids = lax.broadcasted_iota(I32, (PEER_KEYS, slots), 0)

        def gate_bits(t):
            arow = a_s[pl.ds(t, 1), :]
            brow = b_s[pl.ds(t, 1), :]
            grow = g_s[pl.ds(t, 1), :]
            ga = jnp.where(ids == arow, 0.5 * grow, 0.0).astype(BF16)
            ob = jnp.where(ids == brow, 1.0, 0.0).astype(BF16)
            wt = lax.dot_general(ga, ob, _NT, preferred_element_type=F32)
            return lax.bitcast_convert_type(wt, jnp.uint32)

        def token_pairs(i, carry):
            for u in range(BUILD_UNROLL // 2):
                p = i * (BUILD_UNROLL // 2) + u
                packed = (gate_bits(2 * p + 1) & jnp.uint32(HI16)) | (gate_bits(2 * p) >> 16)
                wbuf[pl.ds(pl.multiple_of(p * W_PITCH, 8), PEER_KEYS), :] = packed
            return carry

        lax.fori_loop(0, tb // BUILD_UNROLL, token_pairs, 0)

    cur = lax.rem(c, 2)
    prev = 1 - cur
    cb = jnp.clip(c - 1, 0, n_chunks - 1)
    for j in range(cw):
        cols = slice(j * PEER_KEYS, (j + 1) * PEER_KEYS)
        w = wbuf[pl.ds(cb * cw + j, tb // 2, stride=W_PITCH), :]
        half_gate = pltpu.bitcast(w, BF16)
        x = hb[prev, :, cols]
        u = x * (x * x * GELU_K1 + GELU_K0)
        zs[0, :, cols] = (x * jnp.tanh(u) + x) * half_gate
    acc[...] += jnp.dot(zs[0], wv_ref[...], preferred_element_type=F32)
    hb[cur] = jnp.dot(xn_ref[...], wut_ref[...], preferred_element_type=F32).astype(BF16)

    @pl.when(c == n_chunks)
    def _finish():
        out_ref[...] = _rms(x1_ref[...] + acc[...], gfin_ref[...])


def _peer_call(xn, x1, a_t, b_t, g_t, peer_wu, peer_wv, g_final, tb, cw):
    t, d = xn.shape
    n_exp = peer_wu.shape[0]
    hk = PEER_HEADS * PEER_TOPK
    wut = peer_wu.astype(BF16).T
    wv = peer_wv.astype(BF16)
    n_chunks = PEER_KEYS // cw
    tok = lambda i, c: (i, 0)
    assert ROUTE_TOKENS % tb == 0
    tab = pl.BlockSpec((1, hk * 8, LANES), lambda i, c: (i // (ROUTE_TOKENS // tb), 0, 0))
    return pl.pallas_call(
        functools.partial(_peer_kernel, cw=cw),
        grid=(t // tb, n_chunks + 1),
        in_specs=[
            pl.BlockSpec((tb, d), tok),
            pl.BlockSpec((tb, d), tok),
            tab, tab, tab,
            pl.BlockSpec((d, cw * PEER_KEYS), lambda i, c: (0, jnp.minimum(c, n_chunks - 1))),
            pl.BlockSpec((cw * PEER_KEYS, d), lambda i, c: (jnp.maximum(c - 1, 0), 0)),
            pl.BlockSpec((1, d), lambda i, c: (0, 0)),
        ],
        out_specs=pl.BlockSpec((tb, d), tok),
        out_shape=jax.ShapeDtypeStruct((t, d), F32),
        scratch_shapes=[
            pltpu.VMEM((tb // 2 * W_PITCH, PEER_KEYS), jnp.uint32),
            pltpu.VMEM((tb, d), F32),
            pltpu.VMEM((2, tb, cw * PEER_KEYS), BF16),
            pltpu.VMEM((1, tb, cw * PEER_KEYS), BF16),
            pltpu.VMEM((tb, hk), I32),
            pltpu.VMEM((tb, hk), I32),
            pltpu.VMEM((tb, hk), F32),
        ],
        compiler_params=pltpu.CompilerParams(
            dimension_semantics=("arbitrary", "arbitrary"),
            vmem_limit_bytes=PEER_VMEM_LIMIT_BYTES),
        name="peer_dense",
    )(xn, x1, a_t, b_t, g_t, wut, wv, g_final.reshape(1, d))


def kernel(x, w_in, lam_q1, lam_k1, lam_q2, lam_k2, g_subln, w_s, b_s, g_gv, g_gout,
           w_out, g_mix, g_ffn, peer_wq, peer_keys, peer_wu, peer_wv, g_final):
    batch, seq, d = x.shape
    depth = w_in.shape[0]
    assert depth == 1, "LAM_INIT and the single pass below assume one layer"
    t = batch * seq
    x2 = x.reshape(t, d)
    l = 0
    q, k, v, y = _inproj_call(x2, g_mix[l], w_in[l], w_s[l], b_s[l], g_gv[l], g_gout[l],
                              tm=min(512, t))
    o = _attn_call(q, k, v, lam_q1[l], lam_k1[l], lam_q2[l], lam_k2[l], g_subln[l],
                   batch, seq, qb=min(256, seq))
    x1, xn, st = _outproj_call(o, y, x2, w_out[l], g_ffn[l], peer_wq[l], peer_keys[l],
                               tm=min(512, t))
    a_t, b_t, g_t = _route_call(st)
    out = _peer_call(xn, x1, a_t, b_t, g_t, peer_wu[l], peer_wv[l], g_final,
                     tb=min(512, t), cw=8)
    return out.reshape(batch, seq, d)
```

```python
import functools
import math

import jax
import jax.numpy as jnp
from jax import lax
from jax.experimental import pallas as pl
from jax.experimental.pallas import tpu as pltpu

F32 = jnp.float32
BF16 = jnp.bfloat16
I32 = jnp.int32

EPS = 1e-6
LANES = 128
CHUNK = 64
ATT_HEADS = 4
ATT_HEAD_DIM = 64
ATT_VDIM = 2 * ATT_HEAD_DIM
ATT_WIDTH = ATT_HEADS * ATT_VDIM
GMLP_GROUPS = 4
GMLP_BLOCK = 128
GMLP_GROUP_DIM = 128
GMLP_WIDTH = GMLP_GROUPS * GMLP_GROUP_DIM
PEER_HEADS = 8
PEER_KEYS = 128
KEY_BITS = 7
PEER_TOPK = 16
PEER_HALF = 128
LAM_INIT = 0.8 - 0.6 * math.exp(-0.3 * 0)
ROUTE_TOKENS = 8 * LANES

VMEM_LIMIT_BYTES = 48 * 1024 * 1024
PEER_VMEM_LIMIT_BYTES = 56 * 1024 * 1024

_NT = (((1,), (1,)), ((), ()))


def _rms(x, g):
    return x * lax.rsqrt(jnp.mean(x * x, axis=-1, keepdims=True) + EPS) * g


def _inproj_kernel(x_ref, gmix_ref, win_ref, ws_ref, bsb_ref, ggv_ref, ggout_ref,
                   q_ref, k_ref, v_ref, y_ref):
    tm = x_ref.shape[0]
    h = _rms(x_ref[...], gmix_ref[...]).astype(BF16)

    def proj(lo, hi):
        return jnp.dot(h, win_ref[:, lo:hi], preferred_element_type=F32)

    aw = ATT_WIDTH
    q_ref[...] = (proj(0, aw) * (ATT_HEAD_DIM ** -0.5)).astype(BF16)
    k_ref[...] = proj(aw, 2 * aw).astype(BF16)
    v_ref[...] = proj(2 * aw, 3 * aw).astype(BF16)
    ug = jax.nn.gelu(proj(3 * aw, 3 * aw + GMLP_WIDTH))
    gg = jax.nn.gelu(proj(3 * aw + GMLP_WIDTH, 3 * aw + 2 * GMLP_WIDTH))

    row = lax.broadcasted_iota(I32, (GMLP_BLOCK, GMLP_BLOCK), 0)
    col = lax.broadcasted_iota(I32, (GMLP_BLOCK, GMLP_BLOCK), 1)
    for g in range(GMLP_GROUPS):
        cs = slice(g * GMLP_GROUP_DIM, (g + 1) * GMLP_GROUP_DIM)
        gvn = _rms(gg[:, cs], ggv_ref[:, cs]).astype(BF16)
        w = jnp.where(row >= col, ws_ref[g], 0.0).astype(BF16)
        for blk in range(tm // GMLP_BLOCK):
            rs = slice(blk * GMLP_BLOCK, (blk + 1) * GMLP_BLOCK)
            gate = jnp.dot(w, gvn[rs], preferred_element_type=F32) + bsb_ref[g]
            y_ref[rs, cs] = _rms(ug[rs, cs] * gate, ggout_ref[:, cs]).astype(BF16)


def _inproj_call(x2, g_mix, w_in, w_s, b_s, g_gv, g_gout, tm):
    t, d = x2.shape
    in_w = w_in.shape[1]
    bsb = jnp.broadcast_to(b_s[:, :, None], (GMLP_GROUPS, GMLP_BLOCK, GMLP_GROUP_DIM))
    const2 = lambda i: (0, 0)
    const3 = lambda i: (0, 0, 0)
    tile = lambda i: (i, 0)
    out_sds = jax.ShapeDtypeStruct((t, ATT_WIDTH), BF16)
    return pl.pallas_call(
        _inproj_kernel,
        grid=(t // tm,),
        in_specs=[
            pl.BlockSpec((tm, d), tile),
            pl.BlockSpec((1, d), const2),
            pl.BlockSpec((d, in_w), const2),
            pl.BlockSpec((GMLP_GROUPS, GMLP_BLOCK, GMLP_BLOCK), const3),
            pl.BlockSpec((GMLP_GROUPS, GMLP_BLOCK, GMLP_GROUP_DIM), const3),
            pl.BlockSpec((1, GMLP_WIDTH), const2),
            pl.BlockSpec((1, GMLP_WIDTH), const2),
        ],
        out_specs=[pl.BlockSpec((tm, ATT_WIDTH), tile)] * 4,
        out_shape=[out_sds] * 4,
        compiler_params=pltpu.CompilerParams(
            dimension_semantics=("arbitrary",), vmem_limit_bytes=VMEM_LIMIT_BYTES),
        name="inproj_gmlp",
    )(x2, g_mix.reshape(1, d), w_in.astype(BF16), w_s, bsb,
      g_gv.reshape(1, GMLP_WIDTH), g_gout.reshape(1, GMLP_WIDTH))


def _attn_kernel(slopes_ref, lq1_ref, lk1_ref, lq2_ref, lk2_ref, gsub_ref,
                 q_ref, k_ref, v_ref, o_ref, *, qb):
    seq = q_ref.shape[0]
    slope = slopes_ref[pl.program_id(1)]
    lam = (jnp.exp(jnp.sum(lq1_ref[...] * lk1_ref[...], axis=-1, keepdims=True))
           - jnp.exp(jnp.sum(lq2_ref[...] * lk2_ref[...], axis=-1, keepdims=True))
           + LAM_INIT)
    lane = lax.broadcasted_iota(I32, (qb, ATT_VDIM), 1)
    r = lax.broadcasted_iota(I32, (qb, qb), 0)
    c = lax.broadcasted_iota(I32, (qb, qb), 1)
    diag_bias = jnp.where((c // CHUNK) <= (r // CHUNK),
                          slope * (r - jnp.abs(r - c)).astype(F32), -jnp.inf)
    for qi in range(seq // qb):
        off = qi * qb
        qblk = q_ref[off:off + qb, :]
        kd = k_ref[off:off + qb, :]
        bias_d = diag_bias + slope * off
        if off:
            ko = k_ref[0:off, :]
            bias_o = slope * lax.broadcasted_iota(I32, (1, off), 1).astype(F32)

        def softmax_parts(qm):
            sd = lax.dot_general(qm, kd, _NT, preferred_element_type=F32) + bias_d
            m = jnp.max(sd, axis=-1, keepdims=True)
            if off:
                so = lax.dot_general(qm, ko, _NT, preferred_element_type=F32) + bias_o
                m = jnp.maximum(m, jnp.max(so, axis=-1, keepdims=True))
                po = jnp.exp(so - m)
            pd = jnp.exp(sd - m)
            l = jnp.sum(pd, axis=-1, keepdims=True)
            if off:
                return po, pd, l + jnp.sum(po, axis=-1, keepdims=True)
            return None, pd, l

        po0, pd0, l0 = softmax_parts(jnp.where(lane < ATT_HEAD_DIM, qblk, jnp.zeros_like(qblk)))
        po1, pd1, l1 = softmax_parts(jnp.where(lane >= ATT_HEAD_DIM, qblk, jnp.zeros_like(qblk)))
        w0 = 1.0 / l0
        w1 = lam / l1
        o = jnp.dot((pd0 * w0 - pd1 * w1).astype(BF16), v_ref[off:off + qb, :],
                    preferred_element_type=F32)
        if off:
            o += jnp.dot((po0 * w0 - po1 * w1).astype(BF16), v_ref[0:off, :],
                         preferred_element_type=F32)
        o = _rms(o, gsub_ref[...]) * (1.0 - LAM_INIT)
        o_ref[off:off + qb, :] = o.astype(BF16)


def _attn_call(q, k, v, lam_q1, lam_k1, lam_q2, lam_k2, g_subln, batch, seq, qb):
    t = q.shape[0]
    slopes = 2.0 ** (-8.0 * jnp.arange(1, ATT_HEADS + 1, dtype=F32) / ATT_HEADS)
    blk = pl.BlockSpec((seq, ATT_VDIM), lambda b, h: (b, h))
    vec = lambda n: pl.BlockSpec((1, n), lambda b, h: (0, 0))
    return pl.pallas_call(
        functools.partial(_attn_kernel, qb=qb),
        grid=(batch, ATT_HEADS),
        in_specs=[pl.BlockSpec(memory_space=pltpu.SMEM)]
                 + [vec(ATT_HEAD_DIM)] * 4 + [vec(ATT_VDIM)] + [blk] * 3,
        out_specs=blk,
        out_shape=jax.ShapeDtypeStruct((t, ATT_WIDTH), BF16),
        compiler_params=pltpu.CompilerParams(
            dimension_semantics=("arbitrary", "arbitrary"),
            vmem_limit_bytes=VMEM_LIMIT_BYTES),
        name="diff_attn",
    )(slopes, lam_q1.reshape(1, -1), lam_k1.reshape(1, -1), lam_q2.reshape(1, -1),
      lam_k2.reshape(1, -1), g_subln.reshape(1, -1), q, k, v)


def _outproj_kernel(o_ref, y_ref, x_ref, wo_ref, gffn_ref, wq_ref, keys_ref,
                    x1_ref, xn_ref, st_ref):
    x1 = (x_ref[...]
          + jnp.dot(o_ref[...], wo_ref[0:ATT_WIDTH, :], preferred_element_type=F32)
          + jnp.dot(y_ref[...], wo_ref[ATT_WIDTH:, :], preferred_element_type=F32))
    x1_ref[...] = x1
    xn = _rms(x1, gffn_ref[...]).astype(BF16)
    xn_ref[...] = xn
    qp = jnp.dot(xn, wq_ref[...], preferred_element_type=F32).astype(BF16)
    groups = st_ref.shape[4]
    for hp in range(2 * PEER_HEADS):
        s = lax.dot_general(
            keys_ref[hp], qp[:, hp * PEER_HALF:(hp + 1) * PEER_HALF], _NT,
            preferred_element_type=F32)
        for g in range(groups):
            st_ref[hp // 2, hp % 2, 0, :, g] = s[:, g * LANES:(g + 1) * LANES].reshape(
                PEER_KEYS // 8, 8, LANES)


def _outproj_call(o, y, x2, w_out, g_ffn, peer_wq, peer_keys, tm):
    t, d = x2.shape
    nq = peer_wq.shape[1]
    tile = lambda i: (i, 0)
    const2 = lambda i: (0, 0)
    keys = peer_keys.reshape(2 * PEER_HEADS, PEER_KEYS, PEER_HALF).astype(BF16)
    assert ROUTE_TOKENS % tm == 0 and t % ROUTE_TOKENS == 0
    steps_per_tile = ROUTE_TOKENS // tm
    return pl.pallas_call(
        _outproj_kernel,
        grid=(t // tm,),
        in_specs=[
            pl.BlockSpec((tm, ATT_WIDTH), tile),
            pl.BlockSpec((tm, GMLP_WIDTH), tile),
            pl.BlockSpec((tm, d), tile),
            pl.BlockSpec((ATT_WIDTH + GMLP_WIDTH, d), const2),
            pl.BlockSpec((1, d), const2),
            pl.BlockSpec((d, nq), const2),
            pl.BlockSpec((2 * PEER_HEADS, PEER_KEYS, PEER_HALF), lambda i: (0, 0, 0)),
        ],
        out_specs=[
            pl.BlockSpec((tm, d), tile),
            pl.BlockSpec((tm, d), tile),
            pl.BlockSpec((PEER_HEADS, 2, 1, PEER_KEYS // 8, tm // LANES, 8, LANES),
                         lambda i: (0, 0, i // steps_per_tile, 0, i % steps_per_tile, 0, 0)),
        ],
        out_shape=[
            jax.ShapeDtypeStruct((t, d), F32),
            jax.ShapeDtypeStruct((t, d), BF16),
            jax.ShapeDtypeStruct((PEER_HEADS, 2, t // ROUTE_TOKENS, PEER_KEYS // 8, 8, 8, LANES), F32),
        ],
        compiler_params=pltpu.CompilerParams(
            dimension_semantics=("arbitrary",), vmem_limit_bytes=VMEM_LIMIT_BYTES),
        name="outproj_scores",
    )(o, y, x2, w_out.astype(BF16), g_ffn.reshape(1, d), peer_wq.astype(BF16), keys)


def _extract_topk(s, ids, k):
    big = jnp.iinfo(jnp.int32).max
    vals, picks = [], []
    for _ in range(k):
        m = jnp.max(s, axis=0, keepdims=True)
        pick = jnp.min(jnp.where(s == m, ids, big), axis=0, keepdims=True)
        vals.append(m)
        picks.append(pick)
        s = jnp.where(ids == pick, -jnp.inf, s)
    return jnp.concatenate(vals, axis=0), jnp.concatenate(picks, axis=0)


def _take_rows(table, sel, n):
    out = jnp.zeros(sel.shape, table.dtype)
    for j in range(n):
        out = jnp.where(sel == j, table[j:j + 1], out)
    return out


def _sort_network(n):
    pairs = []

    def merge(lo, length, r):
        step = 2 * r
        if step < length:
            merge(lo, length, step)
            merge(lo + r, length, step)
            pairs.extend((i, i + r) for i in range(lo + r, lo + length - r, step))
        else:
            pairs.append((lo, lo + r))

    def sort(lo, length):
        if length > 1:
            sort(lo, length // 2)
            sort(lo + length // 2, length // 2)
            merge(lo, length, 1)

    sort(0, n)
    return pairs


def _compare_exchange(vals, ids, i, j):
    a, b = vals[i], vals[j]
    ge = a >= b
    vals[i], vals[j] = jnp.maximum(a, b), jnp.minimum(a, b)
    ids[i], ids[j] = jnp.where(ge, ids[i], ids[j]), jnp.where(ge, ids[j], ids[i])


def _bitonic_merge(vals, ids):
    n = len(vals)
    d = n // 2
    while d:
        for i in range(n):
            if (i // d) % 2 == 0:
                _compare_exchange(vals, ids, i, i + d)
        d //= 2


def _merge_top(av, ai, bv, bi):
    n = len(av)
    cv, ci = [], []
    for r in range(n):
        a, b = av[r], bv[n - 1 - r]
        cv.append(jnp.maximum(a, b))
        ci.append(jnp.where(a >= b, ai[r], bi[n - 1 - r]))
    _bitonic_merge(cv, ci)
    return cv, ci


def _merge_all(av, ai, bv, bi):
    cv, ci = av + bv[::-1], ai + bi[::-1]
    _bitonic_merge(cv, ci)
    return cv, ci


def _tie_flag(rows, top, k):
    flag = jnp.zeros(top[0].shape, I32)
    for r in range(k - 1):
        flag = jnp.where(top[r] == top[r + 1], 1, flag)
    count = jnp.zeros(top[0].shape, I32)
    for row in rows:
        count = count + jnp.where(row >= top[k - 1], 1, 0)
    return jnp.where(count != k, 1, flag)


def _net_topk(rows, k):
    pairs = _sort_network(k)
    lists = []
    for g in range(0, len(rows), k):
        vals = list(rows[g:g + k])
        ids = [jnp.full(rows[0].shape, g + j, I32) for j in range(k)]
        for i, j in pairs:
            _compare_exchange(vals, ids, i, j)
        lists.append((vals, ids))
    while len(lists) > 1:
        lists = [_merge_top(*lists[m], *lists[m + 1]) for m in range(0, len(lists), 2)]
    vals, ids = lists[0]
    return vals, ids, _tie_flag(rows, vals, k)


def _route_kernel(st_ref, a_ref, b_ref, g_ref):
    k = PEER_TOPK
    hk = k // 2
    shape = (8, LANES)

    def key_rows(side):
        return [st_ref[0, side, 0, pl.ds((n // 8) * 64 + n % 8, 8, stride=8), :]
                for n in range(PEER_KEYS)]
    widths = [k] + [hk] * (hk - 1)

    def finish(best, expert):
        e = jnp.exp(best - best[0:1])
        a_ref[0] = expert >> KEY_BITS
        b_ref[0] = expert & (PEER_KEYS - 1)
        g_ref[0] = e / jnp.sum(e, axis=0, keepdims=True)

    v1, i1, f1 = _net_topk(key_rows(0), k)
    v2, i2, f2 = _net_topk(key_rows(1), k)
    base = [i * PEER_KEYS for i in i1]
    groups = [([v1[k1] + v2[k2] for k2 in range(w)], [base[k1] + i2[k2] for k2 in range(w)])
              for k1, w in enumerate(widths)]
    groups.append(([v1[k1] + v2[0] for k1 in range(hk, k)], [base[k1] + i2[0] for k1 in range(hk, k)]))
    cand_rows = [v for g in groups for v in g[0]]
    full = [groups[0]] + [_merge_all(*groups[m], *groups[m + 1]) for m in range(1, len(groups), 2)]
    while len(full) > 1:
        nxt = [_merge_top(*full[m], *full[m + 1]) for m in range(0, len(full) - 1, 2)]
        full = nxt + full[len(full) - len(full) % 2:]
    best, expert = full[0]
    finish(jnp.stack(best), jnp.stack(expert))
    tie = jnp.maximum(jnp.maximum(f1, f2), _tie_flag(cand_rows, best, k))

    @pl.when(jnp.max(tie) > 0)
    def _exact():
        key_ids = lax.broadcasted_iota(I32, (PEER_KEYS,) + shape, 0)
        flat = jnp.concatenate(
            [k1 * k + lax.broadcasted_iota(I32, (w,) + shape, 0) for k1, w in enumerate(widths)]
            + [(hk + lax.broadcasted_iota(I32, (hk,) + shape, 0)) * k], axis=0)
        xv1, xi1 = _extract_topk(jnp.stack(key_rows(0)), key_ids, k)
        xv2, xi2 = _extract_topk(jnp.stack(key_rows(1)), key_ids, k)
        cand = jnp.concatenate(
            [xv1[k1:k1 + 1] + xv2[0:w] for k1, w in enumerate(widths)]
            + [xv1[hk:k] + xv2[0:1]], axis=0)
        xbest, pick = _extract_topk(cand, flat, k)
        sel1 = pick // k
        sel2 = pick - sel1 * k
        finish(xbest, _take_rows(xi1, sel1, k) * PEER_KEYS + _take_rows(xi2, sel2, k))


def _route_call(st):
    tiles = st.shape[2]
    slots = PEER_HEADS * PEER_TOPK
    st = st.reshape(PEER_HEADS, 2, tiles, PEER_KEYS * 8, LANES)
    out_spec = pl.BlockSpec((1, PEER_TOPK, 8, LANES), lambda i, h: (i, h, 0, 0))
    out_sds = lambda dt: jax.ShapeDtypeStruct((tiles, slots, 8, LANES), dt)
    outs = pl.pallas_call(
        _route_kernel,
        grid=(tiles, PEER_HEADS),
        in_specs=[pl.BlockSpec((1, 2, 1, PEER_KEYS * 8, LANES), lambda i, h: (h, 0, i, 0, 0))],
        out_specs=[out_spec] * 3,
        out_shape=[out_sds(I32), out_sds(I32), out_sds(F32)],
        compiler_params=pltpu.CompilerParams(
            dimension_semantics=("arbitrary", "arbitrary"),
            vmem_limit_bytes=VMEM_LIMIT_BYTES),
        name="peer_route",
    )(st)
    return [x.reshape(tiles, slots * 8, LANES) for x in outs]


W_PITCH = 136
BUILD_UNROLL = 64
HI16 = 0xFFFF0000
GELU_K0 = math.sqrt(2.0 / math.pi)
GELU_K1 = 0.044715 * GELU_K0


def _peer_kernel(xn_ref, x1_ref, at_ref, bt_ref, gt_ref, wut_ref, wv_ref, gfin_ref,
                 out_ref, wbuf, acc, hb, zs, a_s, b_s, g_s, *, cw):
    tb = xn_ref.shape[0]
    c = pl.program_id(1)
    n_chunks = pl.num_programs(1) - 1
    slots = PEER_HEADS * PEER_TOPK

    @pl.when(c == 0)
    def _build_gates():
        acc[...] = jnp.zeros_like(acc)
        hb[1] = jnp.zeros(hb.shape[1:], hb.dtype)
        g0 = lax.rem(pl.program_id(0), ROUTE_TOKENS // tb) * (tb // LANES)
        for g in range(tb // LANES):
            rows = pl.ds(g * LANES, LANES)
            grp = pl.ds(g0 + g, slots, stride=8)
            a_s[rows, :] = at_ref[0, grp, :].T
            b_s[rows, :] = bt_ref[0, grp, :].T
            g_s[rows, :] = gt_ref[0, grp, :].T
        ids = lax.broadcasted_iota(I32, (PEER_KEYS, slots), 0)

        def gate_bits(t):
            arow = a_s[pl.ds(t, 1), :]
            brow = b_s[pl.ds(t, 1), :]
            grow = g_s[pl.ds(t, 1), :]
            ga = jnp.where(ids == arow, 0.5 * grow, 0.0).astype(BF16)
            ob = jnp.where(ids == brow, 1.0, 0.0).astype(BF16)
            wt = lax.dot_general(ga, ob, _NT, preferred_element_type=F32)
            return lax.bitcast_convert_type(wt, jnp.uint32)

        def token_pairs(i, carry):
            for u in range(BUILD_UNROLL // 2):
                p = i * (BUILD_UNROLL // 2) + u
                packed = (gate_bits(2 * p + 1) & jnp.uint32(HI16)) | (gate_bits(2 * p) >> 16)
                wbuf[pl.ds(pl.multiple_of(p * W_PITCH, 8), PEER_KEYS), :] = packed
            return carry

        lax.fori_loop(0, tb // BUILD_UNROLL, token_pairs, 0)

    cur = lax.rem(c, 2)
    prev = 1 - cur
    cb = jnp.clip(c - 1, 0, n_chunks - 1)
    for j in range(cw):
        cols = slice(j * PEER_KEYS, (j + 1) * PEER_KEYS)
        w = wbuf[pl.ds(cb * cw + j, tb // 2, stride=W_PITCH), :]
        half_gate = pltpu.bitcast(w, BF16)
        x = hb[prev, :, cols]
        u = x * (x * x * GELU_K1 + GELU_K0)
        zs[0, :, cols] = (x * jnp.tanh(u) + x) * half_gate
    acc[...] += jnp.dot(zs[0], wv_ref[...], preferred_element_type=F32)
    hb[cur] = jnp.dot(xn_ref[...], wut_ref[...], preferred_element_type=F32).astype(BF16)

    @pl.when(c == n_chunks)
    def _finish():
        out_ref[...] = _rms(x1_ref[...] + acc[...], gfin_ref[...])


def _peer_call(xn, x1, a_t, b_t, g_t, peer_wu, peer_wv, g_final, tb, cw):
    t, d = xn.shape
    n_exp = peer_wu.shape[0]
    hk = PEER_HEADS * PEER_TOPK
    wut = peer_wu.astype(BF16).T
    wv = peer_wv.astype(BF16)
    n_chunks = PEER_KEYS // cw
    tok = lambda i, c: (i, 0)
    assert ROUTE_TOKENS % tb == 0
    once = pl.Buffered(1)
    tab = pl.BlockSpec((1, hk * 8, LANES), lambda i, c: (i // (ROUTE_TOKENS // tb), 0, 0),
                       pipeline_mode=once)
    return pl.pallas_call(
        functools.partial(_peer_kernel, cw=cw),
        grid=(t // tb, n_chunks + 1),
        in_specs=[
            pl.BlockSpec((tb, d), tok),
            pl.BlockSpec((tb, d), tok, pipeline_mode=once),
            tab, tab, tab,
            pl.BlockSpec((d, cw * PEER_KEYS), lambda i, c: (0, jnp.minimum(c, n_chunks - 1))),
            pl.BlockSpec((cw * PEER_KEYS, d), lambda i, c: (jnp.maximum(c - 1, 0), 0)),
            pl.BlockSpec((1, d), lambda i, c: (0, 0)),
        ],
        out_specs=pl.BlockSpec((tb, d), tok),
        out_shape=jax.ShapeDtypeStruct((t, d), F32),
        scratch_shapes=[
            pltpu.VMEM((tb // 2 * W_PITCH, PEER_KEYS), jnp.uint32),
            pltpu.VMEM((tb, d), F32),
            pltpu.VMEM((2, tb, cw * PEER_KEYS), BF16),
            pltpu.VMEM((1, tb, cw * PEER_KEYS), BF16),
            pltpu.VMEM((tb, hk), I32),
            pltpu.VMEM((tb, hk), I32),
            pltpu.VMEM((tb, hk), F32),
        ],
        compiler_params=pltpu.CompilerParams(
            dimension_semantics=("arbitrary", "arbitrary"),
            vmem_limit_bytes=PEER_VMEM_LIMIT_BYTES),
        name="peer_dense",
    )(xn, x1, a_t, b_t, g_t, wut, wv, g_final.reshape(1, d))


def kernel(x, w_in, lam_q1, lam_k1, lam_q2, lam_k2, g_subln, w_s, b_s, g_gv, g_gout,
           w_out, g_mix, g_ffn, peer_wq, peer_keys, peer_wu, peer_wv, g_final):
    batch, seq, d = x.shape
    depth = w_in.shape[0]
    assert depth == 1, "LAM_INIT and the single pass below assume one layer"
    t = batch * seq
    x2 = x.reshape(t, d)
    l = 0
    q, k, v, y = _inproj_call(x2, g_mix[l], w_in[l], w_s[l], b_s[l], g_gv[l], g_gout[l],
                              tm=min(512, t))
    o = _attn_call(q, k, v, lam_q1[l], lam_k1[l], lam_q2[l], lam_k2[l], g_subln[l],
                   batch, seq, qb=min(256, seq))
    x1, xn, st = _outproj_call(o, y, x2, w_out[l], g_ffn[l], peer_wq[l], peer_keys[l],
                               tm=min(512, t))
    a_t, b_t, g_t = _route_call(st)
    out = _peer_call(xn, x1, a_t, b_t, g_t, peer_wu[l], peer_wv[l], g_final,
                     tb=min(512, t), cw=16)
    return out.reshape(batch, seq, d)
```

```python
import functools
import math

import jax
import jax.numpy as jnp
from jax import lax
from jax.experimental import pallas as pl
from jax.experimental.pallas import tpu as pltpu

F32 = jnp.float32
BF16 = jnp.bfloat16
I32 = jnp.int32

EPS = 1e-6
LANES = 128
CHUNK = 64
ATT_HEADS = 4
ATT_HEAD_DIM = 64
ATT_VDIM = 2 * ATT_HEAD_DIM
ATT_WIDTH = ATT_HEADS * ATT_VDIM
GMLP_GROUPS = 4
GMLP_BLOCK = 128
GMLP_GROUP_DIM = 128
GMLP_WIDTH = GMLP_GROUPS * GMLP_GROUP_DIM
PEER_HEADS = 8
PEER_KEYS = 128
KEY_BITS = 7
PEER_TOPK = 16
PEER_HALF = 128
LAM_INIT = 0.8 - 0.6 * math.exp(-0.3 * 0)
ROUTE_TOKENS = 8 * LANES

VMEM_LIMIT_BYTES = 48 * 1024 * 1024
PEER_VMEM_LIMIT_BYTES = 56 * 1024 * 1024

_NT = (((1,), (1,)), ((), ()))


def _rms(x, g):
    return x * lax.rsqrt(jnp.mean(x * x, axis=-1, keepdims=True) + EPS) * g


def _inproj_kernel(x_ref, gmix_ref, win_ref, ws_ref, bsb_ref, ggv_ref, ggout_ref,
                   q_ref, k_ref, v_ref, y_ref):
    tm = x_ref.shape[0]
    h = _rms(x_ref[...], gmix_ref[...]).astype(BF16)

    def proj(lo, hi):
        return jnp.dot(h, win_ref[:, lo:hi], preferred_element_type=F32)

    aw = ATT_WIDTH
    q_ref[...] = (proj(0, aw) * (ATT_HEAD_DIM ** -0.5)).astype(BF16)
    k_ref[...] = proj(aw, 2 * aw).astype(BF16)
    v_ref[...] = proj(2 * aw, 3 * aw).astype(BF16)
    ug = jax.nn.gelu(proj(3 * aw, 3 * aw + GMLP_WIDTH))
    gg = jax.nn.gelu(proj(3 * aw + GMLP_WIDTH, 3 * aw + 2 * GMLP_WIDTH))

    row = lax.broadcasted_iota(I32, (GMLP_BLOCK, GMLP_BLOCK), 0)
    col = lax.broadcasted_iota(I32, (GMLP_BLOCK, GMLP_BLOCK), 1)
    for g in range(GMLP_GROUPS):
        cs = slice(g * GMLP_GROUP_DIM, (g + 1) * GMLP_GROUP_DIM)
        gvn = _rms(gg[:, cs], ggv_ref[:, cs]).astype(BF16)
        w = jnp.where(row >= col, ws_ref[g], 0.0).astype(BF16)
        for blk in range(tm // GMLP_BLOCK):
            rs = slice(blk * GMLP_BLOCK, (blk + 1) * GMLP_BLOCK)
            gate = jnp.dot(w, gvn[rs], preferred_element_type=F32) + bsb_ref[g]
            y_ref[rs, cs] = _rms(ug[rs, cs] * gate, ggout_ref[:, cs]).astype(BF16)


def _inproj_call(x2, g_mix, w_in, w_s, b_s, g_gv, g_gout, tm):
    t, d = x2.shape
    in_w = w_in.shape[1]
    bsb = jnp.broadcast_to(b_s[:, :, None], (GMLP_GROUPS, GMLP_BLOCK, GMLP_GROUP_DIM))
    const2 = lambda i: (0, 0)
    const3 = lambda i: (0, 0, 0)
    tile = lambda i: (i, 0)
    out_sds = jax.ShapeDtypeStruct((t, ATT_WIDTH), BF16)
    return pl.pallas_call(
        _inproj_kernel,
        grid=(t // tm,),
        in_specs=[
            pl.BlockSpec((tm, d), tile),
            pl.BlockSpec((1, d), const2),
            pl.BlockSpec((d, in_w), const2),
            pl.BlockSpec((GMLP_GROUPS, GMLP_BLOCK, GMLP_BLOCK), const3),
            pl.BlockSpec((GMLP_GROUPS, GMLP_BLOCK, GMLP_GROUP_DIM), const3),
            pl.BlockSpec((1, GMLP_WIDTH), const2),
            pl.BlockSpec((1, GMLP_WIDTH), const2),
        ],
        out_specs=[pl.BlockSpec((tm, ATT_WIDTH), tile)] * 4,
        out_shape=[out_sds] * 4,
        compiler_params=pltpu.CompilerParams(
            dimension_semantics=("arbitrary",), vmem_limit_bytes=VMEM_LIMIT_BYTES),
        name="inproj_gmlp",
    )(x2, g_mix.reshape(1, d), w_in.astype(BF16), w_s, bsb,
      g_gv.reshape(1, GMLP_WIDTH), g_gout.reshape(1, GMLP_WIDTH))


def _attn_kernel(slopes_ref, lq1_ref, lk1_ref, lq2_ref, lk2_ref, gsub_ref,
                 q_ref, k_ref, v_ref, o_ref, *, qb):
    seq = q_ref.shape[0]
    slope = slopes_ref[pl.program_id(1)]
    lam = (jnp.exp(jnp.sum(lq1_ref[...] * lk1_ref[...], axis=-1, keepdims=True))
           - jnp.exp(jnp.sum(lq2_ref[...] * lk2_ref[...], axis=-1, keepdims=True))
           + LAM_INIT)
    lane = lax.broadcasted_iota(I32, (qb, ATT_VDIM), 1)
    r = lax.broadcasted_iota(I32, (qb, qb), 0)
    c = lax.broadcasted_iota(I32, (qb, qb), 1)
    diag_bias = jnp.where((c // CHUNK) <= (r // CHUNK),
                          slope * (r - jnp.abs(r - c)).astype(F32), -jnp.inf)
    for qi in range(seq // qb):
        off = qi * qb
        qblk = q_ref[off:off + qb, :]
        kd = k_ref[off:off + qb, :]
        bias_d = diag_bias + slope * off
        if off:
            ko = k_ref[0:off, :]
            bias_o = slope * lax.broadcasted_iota(I32, (1, off), 1).astype(F32)

        def softmax_parts(qm):
            sd = lax.dot_general(qm, kd, _NT, preferred_element_type=F32) + bias_d
            m = jnp.max(sd, axis=-1, keepdims=True)
            if off:
                so = lax.dot_general(qm, ko, _NT, preferred_element_type=F32) + bias_o
                m = jnp.maximum(m, jnp.max(so, axis=-1, keepdims=True))
                po = jnp.exp(so - m)
            pd = jnp.exp(sd - m)
            l = jnp.sum(pd, axis=-1, keepdims=True)
            if off:
                return po, pd, l + jnp.sum(po, axis=-1, keepdims=True)
            return None, pd, l

        po0, pd0, l0 = softmax_parts(jnp.where(lane < ATT_HEAD_DIM, qblk, jnp.zeros_like(qblk)))
        po1, pd1, l1 = softmax_parts(jnp.where(lane >= ATT_HEAD_DIM, qblk, jnp.zeros_like(qblk)))
        w0 = 1.0 / l0
        w1 = lam / l1
        o = jnp.dot((pd0 * w0 - pd1 * w1).astype(BF16), v_ref[off:off + qb, :],
                    preferred_element_type=F32)
        if off:
            o += jnp.dot((po0 * w0 - po1 * w1).astype(BF16), v_ref[0:off, :],
                         preferred_element_type=F32)
        o = _rms(o, gsub_ref[...]) * (1.0 - LAM_INIT)
        o_ref[off:off + qb, :] = o.astype(BF16)


def _attn_call(q, k, v, lam_q1, lam_k1, lam_q2, lam_k2, g_subln, batch, seq, qb):
    t = q.shape[0]
    slopes = 2.0 ** (-8.0 * jnp.arange(1, ATT_HEADS + 1, dtype=F32) / ATT_HEADS)
    blk = pl.BlockSpec((seq, ATT_VDIM), lambda b, h: (b, h))
    vec = lambda n: pl.BlockSpec((1, n), lambda b, h: (0, 0))
    return pl.pallas_call(
        functools.partial(_attn_kernel, qb=qb),
        grid=(batch, ATT_HEADS),
        in_specs=[pl.BlockSpec(memory_space=pltpu.SMEM)]
                 + [vec(ATT_HEAD_DIM)] * 4 + [vec(ATT_VDIM)] + [blk] * 3,
        out_specs=blk,
        out_shape=jax.ShapeDtypeStruct((t, ATT_WIDTH), BF16),
        compiler_params=pltpu.CompilerParams(
            dimension_semantics=("arbitrary", "arbitrary"),
            vmem_limit_bytes=VMEM_LIMIT_BYTES),
        name="diff_attn",
    )(slopes, lam_q1.reshape(1, -1), lam_k1.reshape(1, -1), lam_q2.reshape(1, -1),
      lam_k2.reshape(1, -1), g_subln.reshape(1, -1), q, k, v)


def _outproj_kernel(o_ref, y_ref, x_ref, wo_ref, gffn_ref, wq_ref, keys_ref,
                    x1_ref, xn_ref, st_ref):
    x1 = (x_ref[...]
          + jnp.dot(o_ref[...], wo_ref[0:ATT_WIDTH, :], preferred_element_type=F32)
          + jnp.dot(y_ref[...], wo_ref[ATT_WIDTH:, :], preferred_element_type=F32))
    x1_ref[...] = x1
    xn = _rms(x1, gffn_ref[...]).astype(BF16)
    xn_ref[...] = xn
    qp = jnp.dot(xn, wq_ref[...], preferred_element_type=F32).astype(BF16)
    groups = st_ref.shape[4]
    for hp in range(2 * PEER_HEADS):
        s = lax.dot_general(
            keys_ref[hp], qp[:, hp * PEER_HALF:(hp + 1) * PEER_HALF], _NT,
            preferred_element_type=F32)
        for g in range(groups):
            st_ref[hp // 2, hp % 2, 0, :, g] = s[:, g * LANES:(g + 1) * LANES].reshape(
                PEER_KEYS // 8, 8, LANES)


def _outproj_call(o, y, x2, w_out, g_ffn, peer_wq, peer_keys, tm):
    t, d = x2.shape
    nq = peer_wq.shape[1]
    tile = lambda i: (i, 0)
    const2 = lambda i: (0, 0)
    keys = peer_keys.reshape(2 * PEER_HEADS, PEER_KEYS, PEER_HALF).astype(BF16)
    assert ROUTE_TOKENS % tm == 0 and t % ROUTE_TOKENS == 0
    steps_per_tile = ROUTE_TOKENS // tm
    return pl.pallas_call(
        _outproj_kernel,
        grid=(t // tm,),
        in_specs=[
            pl.BlockSpec((tm, ATT_WIDTH), tile),
            pl.BlockSpec((tm, GMLP_WIDTH), tile),
            pl.BlockSpec((tm, d), tile),
            pl.BlockSpec((ATT_WIDTH + GMLP_WIDTH, d), const2),
            pl.BlockSpec((1, d), const2),
            pl.BlockSpec((d, nq), const2),
            pl.BlockSpec((2 * PEER_HEADS, PEER_KEYS, PEER_HALF), lambda i: (0, 0, 0)),
        ],
        out_specs=[
            pl.BlockSpec((tm, d), tile),
            pl.BlockSpec((tm, d), tile),
            pl.BlockSpec((PEER_HEADS, 2, 1, PEER_KEYS // 8, tm // LANES, 8, LANES),
                         lambda i: (0, 0, i // steps_per_tile, 0, i % steps_per_tile, 0, 0)),
        ],
        out_shape=[
            jax.ShapeDtypeStruct((t, d), F32),
            jax.ShapeDtypeStruct((t, d), BF16),
            jax.ShapeDtypeStruct((PEER_HEADS, 2, t // ROUTE_TOKENS, PEER_KEYS // 8, 8, 8, LANES), F32),
        ],
        compiler_params=pltpu.CompilerParams(
            dimension_semantics=("arbitrary",), vmem_limit_bytes=VMEM_LIMIT_BYTES),
        name="outproj_scores",
    )(o, y, x2, w_out.astype(BF16), g_ffn.reshape(1, d), peer_wq.astype(BF16), keys)


def _extract_topk(s, ids, k):
    big = jnp.iinfo(jnp.int32).max
    vals, picks = [], []
    for _ in range(k):
        m = jnp.max(s, axis=0, keepdims=True)
        pick = jnp.min(jnp.where(s == m, ids, big), axis=0, keepdims=True)
        vals.append(m)
        picks.append(pick)
        s = jnp.where(ids == pick, -jnp.inf, s)
    return jnp.concatenate(vals, axis=0), jnp.concatenate(picks, axis=0)


def _take_rows(table, sel, n):
    out = jnp.zeros(sel.shape, table.dtype)
    for j in range(n):
        out = jnp.where(sel == j, table[j:j + 1], out)
    return out


def _sort_network(n):
    pairs = []

    def merge(lo, length, r):
        step = 2 * r
        if step < length:
            merge(lo, length, step)
            merge(lo + r, length, step)
            pairs.extend((i, i + r) for i in range(lo + r, lo + length - r, step))
        else:
            pairs.append((lo, lo + r))

    def sort(lo, length):
        if length > 1:
            sort(lo, length // 2)
            sort(lo + length // 2, length // 2)
            merge(lo, length, 1)

    sort(0, n)
    return pairs


def _compare_exchange(vals, ids, i, j):
    a, b = vals[i], vals[j]
    ge = a >= b
    vals[i], vals[j] = jnp.maximum(a, b), jnp.minimum(a, b)
    ids[i], ids[j] = jnp.where(ge, ids[i], ids[j]), jnp.where(ge, ids[j], ids[i])


def _bitonic_merge(vals, ids):
    n = len(vals)
    d = n // 2
    while d:
        for i in range(n):
            if (i // d) % 2 == 0:
                _compare_exchange(vals, ids, i, i + d)
        d //= 2


def _merge_top(av, ai, bv, bi):
    n = len(av)
    cv, ci = [], []
    for r in range(n):
        a, b = av[r], bv[n - 1 - r]
        cv.append(jnp.maximum(a, b))
        ci.append(jnp.where(a >= b, ai[r], bi[n - 1 - r]))
    _bitonic_merge(cv, ci)
    return cv, ci


def _merge_all(av, ai, bv, bi):
    cv, ci = av + bv[::-1], ai + bi[::-1]
    _bitonic_merge(cv, ci)
    return cv, ci


def _tie_flag(rows, top, k):
    flag = jnp.zeros(top[0].shape, I32)
    for r in range(k - 1):
        flag = jnp.where(top[r] == top[r + 1], 1, flag)
    count = jnp.zeros(top[0].shape, I32)
    for row in rows:
        count = count + jnp.where(row >= top[k - 1], 1, 0)
    return jnp.where(count != k, 1, flag)


def _net_topk(rows, k):
    pairs = _sort_network(k)
    lists = []
    for g in range(0, len(rows), k):
        vals = list(rows[g:g + k])
        ids = [jnp.full(rows[0].shape, g + j, I32) for j in range(k)]
        for i, j in pairs:
            _compare_exchange(vals, ids, i, j)
        lists.append((vals, ids))
    while len(lists) > 1:
        lists = [_merge_top(*lists[m], *lists[m + 1]) for m in range(0, len(lists), 2)]
    vals, ids = lists[0]
    return vals, ids, _tie_flag(rows, vals, k)


def _route_kernel(st_ref, a_ref, b_ref, g_ref):
    k = PEER_TOPK
    hk = k // 2
    shape = (8, LANES)

    def key_rows(side):
        return [st_ref[0, side, 0, pl.ds((n // 8) * 64 + n % 8, 8, stride=8), :]
                for n in range(PEER_KEYS)]
    widths = [k] + [hk] * (hk - 1)

    def finish(best, expert):
        e = jnp.exp(best - best[0:1])
        a_ref[0] = expert >> KEY_BITS
        b_ref[0] = expert & (PEER_KEYS - 1)
        g_ref[0] = e / jnp.sum(e, axis=0, keepdims=True)

    v1, i1, f1 = _net_topk(key_rows(0), k)
    v2, i2, f2 = _net_topk(key_rows(1), k)
    base = [i * PEER_KEYS for i in i1]
    groups = [([v1[k1] + v2[k2] for k2 in range(w)], [base[k1] + i2[k2] for k2 in range(w)])
              for k1, w in enumerate(widths)]
    groups.append(([v1[k1] + v2[0] for k1 in range(hk, k)], [base[k1] + i2[0] for k1 in range(hk, k)]))
    cand_rows = [v for g in groups for v in g[0]]
    full = [groups[0]] + [_merge_all(*groups[m], *groups[m + 1]) for m in range(1, len(groups), 2)]
    while len(full) > 1:
        nxt = [_merge_top(*full[m], *full[m + 1]) for m in range(0, len(full) - 1, 2)]
        full = nxt + full[len(full) - len(full) % 2:]
    best, expert = full[0]
    finish(jnp.stack(best), jnp.stack(expert))
    tie = jnp.maximum(jnp.maximum(f1, f2), _tie_flag(cand_rows, best, k))

    @pl.when(jnp.max(tie) > 0)
    def _exact():
        key_ids = lax.broadcasted_iota(I32, (PEER_KEYS,) + shape, 0)
        flat = jnp.concatenate(
            [k1 * k + lax.broadcasted_iota(I32, (w,) + shape, 0) for k1, w in enumerate(widths)]
            + [(hk + lax.broadcasted_iota(I32, (hk,) + shape, 0)) * k], axis=0)
        xv1, xi1 = _extract_topk(jnp.stack(key_rows(0)), key_ids, k)
        xv2, xi2 = _extract_topk(jnp.stack(key_rows(1)), key_ids, k)
        cand = jnp.concatenate(
            [xv1[k1:k1 + 1] + xv2[0:w] for k1, w in enumerate(widths)]
            + [xv1[hk:k] + xv2[0:1]], axis=0)
        xbest, pick = _extract_topk(cand, flat, k)
        sel1 = pick // k
        sel2 = pick - sel1 * k
        finish(xbest, _take_rows(xi1, sel1, k) * PEER_KEYS + _take_rows(xi2, sel2, k))


def _route_call(st):
    tiles = st.shape[2]
    slots = PEER_HEADS * PEER_TOPK
    st = st.reshape(PEER_HEADS, 2, tiles, PEER_KEYS * 8, LANES)
    out_spec = pl.BlockSpec((1, PEER_TOPK, 8, LANES), lambda i, h: (i, h, 0, 0))
    out_sds = lambda dt: jax.ShapeDtypeStruct((tiles, slots, 8, LANES), dt)
    outs = pl.pallas_call(
        _route_kernel,
        grid=(tiles, PEER_HEADS),
        in_specs=[pl.BlockSpec((1, 2, 1, PEER_KEYS * 8, LANES), lambda i, h: (h, 0, i, 0, 0))],
        out_specs=[out_spec] * 3,
        out_shape=[out_sds(I32), out_sds(I32), out_sds(F32)],
        compiler_params=pltpu.CompilerParams(
            dimension_semantics=("arbitrary", "arbitrary"),
            vmem_limit_bytes=VMEM_LIMIT_BYTES),
        name="peer_route",
    )(st)
    return [x.reshape(tiles, slots * 8, LANES) for x in outs]


W_PITCH = 136
BUILD_UNROLL = 64
HI16 = 0xFFFF0000
GELU_K0 = math.sqrt(2.0 / math.pi)
GELU_K1 = 0.044715 * GELU_K0


def _peer_kernel(xn_ref, x1_ref, at_ref, bt_ref, gt_ref, wut_ref, wv_ref, gfin_ref,
                 out_ref, wbuf, acc, hb, zs, a_s, b_s, g_s, *, cw, n_chunks, n_blocks):
    tb = xn_ref.shape[0]
    s = pl.program_id(0)
    blk = s // n_chunks
    c = s % n_chunks
    slots = PEER_HEADS * PEER_TOPK

    @pl.when(s == 0)
    def _init():
        acc[...] = jnp.zeros_like(acc)
        hb[1] = jnp.zeros(hb.shape[1:], hb.dtype)
        wbuf[...] = jnp.zeros_like(wbuf)

    def build_gates():
        g0 = lax.rem(blk, ROUTE_TOKENS // tb) * (tb // LANES)
        for g in range(tb // LANES):
            rows = pl.ds(g * LANES, LANES)
            grp = pl.ds(g0 + g, slots, stride=8)
            a_s[rows, :] = at_ref[0, grp, :].T
            b_s[rows, :] = bt_ref[0, grp, :].T
            g_s[rows, :] = gt_ref[0, grp, :].T
        ids = lax.broadcasted_iota(I32, (PEER_KEYS, slots), 0)

        def gate_bits(t):
            arow = a_s[pl.ds(t, 1), :]
            brow = b_s[pl.ds(t, 1), :]
            grow = g_s[pl.ds(t, 1), :]
            ga = jnp.where(ids == arow, 0.5 * grow, 0.0).astype(BF16)
            ob = jnp.where(ids == brow, 1.0, 0.0).astype(BF16)
            wt = lax.dot_general(ga, ob, _NT, preferred_element_type=F32)
            return lax.bitcast_convert_type(wt, jnp.uint32)

        def token_pairs(i, carry):
            for u in range(BUILD_UNROLL // 2):
                p = i * (BUILD_UNROLL // 2) + u
                packed = (gate_bits(2 * p + 1) & jnp.uint32(HI16)) | (gate_bits(2 * p) >> 16)
                wbuf[pl.ds(pl.multiple_of(p * W_PITCH, 8), PEER_KEYS), :] = packed
            return carry

        lax.fori_loop(0, tb // BUILD_UNROLL, token_pairs, 0)

    cur = lax.rem(s, 2)
    prev = 1 - cur
    cb = lax.rem(c + n_chunks - 1, n_chunks)
    for j in range(cw):
        cols = slice(j * PEER_KEYS, (j + 1) * PEER_KEYS)
        w = wbuf[pl.ds(cb * cw + j, tb // 2, stride=W_PITCH), :]
        half_gate = pltpu.bitcast(w, BF16)
        x = hb[prev, :, cols]
        u = x * (x * x * GELU_K1 + GELU_K0)
        zs[0, :, cols] = (x * jnp.tanh(u) + x) * half_gate
    acc[...] += jnp.dot(zs[0], wv_ref[...], preferred_element_type=F32)
    hb[cur] = jnp.dot(xn_ref[...], wut_ref[...], preferred_element_type=F32).astype(BF16)

    @pl.when(c == 0)
    def _block_boundary():
        out_ref[...] = _rms(x1_ref[...] + acc[...], gfin_ref[...])
        acc[...] = jnp.zeros_like(acc)

        @pl.when(blk < n_blocks)
        def _():
            build_gates()


def _peer_call(xn, x1, a_t, b_t, g_t, peer_wu, peer_wv, g_final, tb, cw):
    t, d = xn.shape
    n_exp = peer_wu.shape[0]
    hk = PEER_HEADS * PEER_TOPK
    wut = peer_wu.astype(BF16).T
    wv = peer_wv.astype(BF16)
    n_chunks = PEER_KEYS // cw
    n_blocks = t // tb
    assert ROUTE_TOKENS % tb == 0
    last = n_blocks - 1
    blk_of = lambda s: jnp.minimum(s // n_chunks, last)
    prev_blk_of = lambda s: jnp.clip((s - 1) // n_chunks, 0, last)
    tab = pl.BlockSpec((1, hk * 8, LANES), lambda s: (blk_of(s) // (ROUTE_TOKENS // tb), 0, 0))
    return pl.pallas_call(
        functools.partial(_peer_kernel, cw=cw, n_chunks=n_chunks, n_blocks=n_blocks),
        grid=(n_blocks * n_chunks + 1,),
        in_specs=[
            pl.BlockSpec((tb, d), lambda s: (blk_of(s), 0)),
            pl.BlockSpec((tb, d), lambda s: (prev_blk_of(s), 0)),
            tab, tab, tab,
            pl.BlockSpec((d, cw * PEER_KEYS), lambda s: (0, s % n_chunks)),
            pl.BlockSpec((cw * PEER_KEYS, d), lambda s: ((s + n_chunks - 1) % n_chunks, 0)),
            pl.BlockSpec((1, d), lambda s: (0, 0)),
        ],
        out_specs=pl.BlockSpec((tb, d), lambda s: (prev_blk_of(s), 0)),
        out_shape=jax.ShapeDtypeStruct((t, d), F32),
        scratch_shapes=[
            pltpu.VMEM((tb // 2 * W_PITCH, PEER_KEYS), jnp.uint32),
            pltpu.VMEM((tb, d), F32),
            pltpu.VMEM((2, tb, cw * PEER_KEYS), BF16),
            pltpu.VMEM((1, tb, cw * PEER_KEYS), BF16),
            pltpu.VMEM((tb, hk), I32),
            pltpu.VMEM((tb, hk), I32),
            pltpu.VMEM((tb, hk), F32),
        ],
        compiler_params=pltpu.CompilerParams(
            dimension_semantics=("arbitrary",),
            vmem_limit_bytes=PEER_VMEM_LIMIT_BYTES),
        name="peer_dense",
    )(xn, x1, a_t, b_t, g_t, wut, wv, g_final.reshape(1, d))


def kernel(x, w_in, lam_q1, lam_k1, lam_q2, lam_k2, g_subln, w_s, b_s, g_gv, g_gout,
           w_out, g_mix, g_ffn, peer_wq, peer_keys, peer_wu, peer_wv, g_final):
    batch, seq, d = x.shape
    depth = w_in.shape[0]
    assert depth == 1, "LAM_INIT and the single pass below assume one layer"
    t = batch * seq
    x2 = x.reshape(t, d)
    l = 0
    q, k, v, y = _inproj_call(x2, g_mix[l], w_in[l], w_s[l], b_s[l], g_gv[l], g_gout[l],
                              tm=min(512, t))
    o = _attn_call(q, k, v, lam_q1[l], lam_k1[l], lam_q2[l], lam_k2[l], g_subln[l],
                   batch, seq, qb=min(256, seq))
    x1, xn, st = _outproj_call(o, y, x2, w_out[l], g_ffn[l], peer_wq[l], peer_keys[l],
                               tm=min(512, t))
    a_t, b_t, g_t = _route_call(st)
    out = _peer_call(xn, x1, a_t, b_t, g_t, peer_wu[l], peer_wv[l], g_final,
                     tb=min(512, t), cw=8)
    return out.reshape(batch, seq, d)
```

```python
import functools
import math

import jax
import jax.numpy as jnp
from jax import lax
from jax.experimental import pallas as pl
from jax.experimental.pallas import tpu as pltpu

F32 = jnp.float32
BF16 = jnp.bfloat16
I32 = jnp.int32

EPS = 1e-6
LANES = 128
CHUNK = 64
ATT_HEADS = 4
ATT_HEAD_DIM = 64
ATT_VDIM = 2 * ATT_HEAD_DIM
ATT_WIDTH = ATT_HEADS * ATT_VDIM
GMLP_GROUPS = 4
GMLP_BLOCK = 128
GMLP_GROUP_DIM = 128
GMLP_WIDTH = GMLP_GROUPS * GMLP_GROUP_DIM
PEER_HEADS = 8
PEER_KEYS = 128
KEY_BITS = 7
PEER_TOPK = 16
PEER_HALF = 128
LAM_INIT = 0.8 - 0.6 * math.exp(-0.3 * 0)
ROUTE_TOKENS = 8 * LANES

VMEM_LIMIT_BYTES = 48 * 1024 * 1024
PEER_VMEM_LIMIT_BYTES = 56 * 1024 * 1024

_NT = (((1,), (1,)), ((), ()))


def _rms(x, g):
    return x * lax.rsqrt(jnp.mean(x * x, axis=-1, keepdims=True) + EPS) * g


def _inproj_kernel(x_ref, gmix_ref, win_ref, ws_ref, bsb_ref, ggv_ref, ggout_ref,
                   q_ref, k_ref, v_ref, y_ref):
    tm = x_ref.shape[0]
    h = _rms(x_ref[...], gmix_ref[...]).astype(BF16)

    def proj(lo, hi):
        return jnp.dot(h, win_ref[:, lo:hi], preferred_element_type=F32)

    aw = ATT_WIDTH
    q_ref[...] = (proj(0, aw) * (ATT_HEAD_DIM ** -0.5)).astype(BF16)
    k_ref[...] = proj(aw, 2 * aw).astype(BF16)
    v_ref[...] = proj(2 * aw, 3 * aw).astype(BF16)
    ug = jax.nn.gelu(proj(3 * aw, 3 * aw + GMLP_WIDTH))
    gg = jax.nn.gelu(proj(3 * aw + GMLP_WIDTH, 3 * aw + 2 * GMLP_WIDTH))

    row = lax.broadcasted_iota(I32, (GMLP_BLOCK, GMLP_BLOCK), 0)
    col = lax.broadcasted_iota(I32, (GMLP_BLOCK, GMLP_BLOCK), 1)
    for g in range(GMLP_GROUPS):
        cs = slice(g * GMLP_GROUP_DIM, (g + 1) * GMLP_GROUP_DIM)
        gvn = _rms(gg[:, cs], ggv_ref[:, cs]).astype(BF16)
        w = jnp.where(row >= col, ws_ref[g], 0.0).astype(BF16)
        for blk in range(tm // GMLP_BLOCK):
            rs = slice(blk * GMLP_BLOCK, (blk + 1) * GMLP_BLOCK)
            gate = jnp.dot(w, gvn[rs], preferred_element_type=F32) + bsb_ref[g]
            y_ref[rs, cs] = _rms(ug[rs, cs] * gate, ggout_ref[:, cs]).astype(BF16)


def _inproj_call(x2, g_mix, w_in, w_s, b_s, g_gv, g_gout, tm):
    t, d = x2.shape
    in_w = w_in.shape[1]
    bsb = jnp.broadcast_to(b_s[:, :, None], (GMLP_GROUPS, GMLP_BLOCK, GMLP_GROUP_DIM))
    const2 = lambda i: (0, 0)
    const3 = lambda i: (0, 0, 0)
    tile = lambda i: (i, 0)
    out_sds = jax.ShapeDtypeStruct((t, ATT_WIDTH), BF16)
    return pl.pallas_call(
        _inproj_kernel,
        grid=(t // tm,),
        in_specs=[
            pl.BlockSpec((tm, d), tile),
            pl.BlockSpec((1, d), const2),
            pl.BlockSpec((d, in_w), const2),
            pl.BlockSpec((GMLP_GROUPS, GMLP_BLOCK, GMLP_BLOCK), const3),
            pl.BlockSpec((GMLP_GROUPS, GMLP_BLOCK, GMLP_GROUP_DIM), const3),
            pl.BlockSpec((1, GMLP_WIDTH), const2),
            pl.BlockSpec((1, GMLP_WIDTH), const2),
        ],
        out_specs=[pl.BlockSpec((tm, ATT_WIDTH), tile)] * 4,
        out_shape=[out_sds] * 4,
        compiler_params=pltpu.CompilerParams(
            dimension_semantics=("arbitrary",), vmem_limit_bytes=VMEM_LIMIT_BYTES),
        name="inproj_gmlp",
    )(x2, g_mix.reshape(1, d), w_in.astype(BF16), w_s, bsb,
      g_gv.reshape(1, GMLP_WIDTH), g_gout.reshape(1, GMLP_WIDTH))


def _attn_kernel(slopes_ref, lq1_ref, lk1_ref, lq2_ref, lk2_ref, gsub_ref,
                 q_ref, k_ref, v_ref, o_ref, *, qb):
    seq = q_ref.shape[0]
    slope = slopes_ref[pl.program_id(1)]
    lam = (jnp.exp(jnp.sum(lq1_ref[...] * lk1_ref[...], axis=-1, keepdims=True))
           - jnp.exp(jnp.sum(lq2_ref[...] * lk2_ref[...], axis=-1, keepdims=True))
           + LAM_INIT)
    lane = lax.broadcasted_iota(I32, (qb, ATT_VDIM), 1)
    r = lax.broadcasted_iota(I32, (qb, qb), 0)
    c = lax.broadcasted_iota(I32, (qb, qb), 1)
    diag_bias = jnp.where((c // CHUNK) <= (r // CHUNK),
                          slope * (r - jnp.abs(r - c)).astype(F32), -jnp.inf)
    for qi in range(seq // qb):
        off = qi * qb
        qblk = q_ref[off:off + qb, :]
        kd = k_ref[off:off + qb, :]
        bias_d = diag_bias + slope * off
        if off:
            ko = k_ref[0:off, :]
            bias_o = slope * lax.broadcasted_iota(I32, (1, off), 1).astype(F32)

        def softmax_parts(qm):
            sd = lax.dot_general(qm, kd, _NT, preferred_element_type=F32) + bias_d
            m = jnp.max(sd, axis=-1, keepdims=True)
            if off:
                so = lax.dot_general(qm, ko, _NT, preferred_element_type=F32) + bias_o
                m = jnp.maximum(m, jnp.max(so, axis=-1, keepdims=True))
                po = jnp.exp(so - m)
            pd = jnp.exp(sd - m)
            l = jnp.sum(pd, axis=-1, keepdims=True)
            if off:
                return po, pd, l + jnp.sum(po, axis=-1, keepdims=True)
            return None, pd, l

        po0, pd0, l0 = softmax_parts(jnp.where(lane < ATT_HEAD_DIM, qblk, jnp.zeros_like(qblk)))
        po1, pd1, l1 = softmax_parts(jnp.where(lane >= ATT_HEAD_DIM, qblk, jnp.zeros_like(qblk)))
        w0 = 1.0 / l0
        w1 = lam / l1
        o = jnp.dot((pd0 * w0 - pd1 * w1).astype(BF16), v_ref[off:off + qb, :],
                    preferred_element_type=F32)
        if off:
            o += jnp.dot((po0 * w0 - po1 * w1).astype(BF16), v_ref[0:off, :],
                         preferred_element_type=F32)
        o = _rms(o, gsub_ref[...]) * (1.0 - LAM_INIT)
        o_ref[off:off + qb, :] = o.astype(BF16)


def _attn_call(q, k, v, lam_q1, lam_k1, lam_q2, lam_k2, g_subln, batch, seq, qb):
    t = q.shape[0]
    slopes = 2.0 ** (-8.0 * jnp.arange(1, ATT_HEADS + 1, dtype=F32) / ATT_HEADS)
    blk = pl.BlockSpec((seq, ATT_VDIM), lambda b, h: (b, h))
    vec = lambda n: pl.BlockSpec((1, n), lambda b, h: (0, 0))
    return pl.pallas_call(
        functools.partial(_attn_kernel, qb=qb),
        grid=(batch, ATT_HEADS),
        in_specs=[pl.BlockSpec(memory_space=pltpu.SMEM)]
                 + [vec(ATT_HEAD_DIM)] * 4 + [vec(ATT_VDIM)] + [blk] * 3,
        out_specs=blk,
        out_shape=jax.ShapeDtypeStruct((t, ATT_WIDTH), BF16),
        compiler_params=pltpu.CompilerParams(
            dimension_semantics=("arbitrary", "arbitrary"),
            vmem_limit_bytes=VMEM_LIMIT_BYTES),
        name="diff_attn",
    )(slopes, lam_q1.reshape(1, -1), lam_k1.reshape(1, -1), lam_q2.reshape(1, -1),
      lam_k2.reshape(1, -1), g_subln.reshape(1, -1), q, k, v)


def _outproj_kernel(o_ref, y_ref, x_ref, wo_ref, gffn_ref, wq_ref, keys_ref,
                    x1_ref, xn_ref, st_ref):
    x1 = (x_ref[...]
          + jnp.dot(o_ref[...], wo_ref[0:ATT_WIDTH, :], preferred_element_type=F32)
          + jnp.dot(y_ref[...], wo_ref[ATT_WIDTH:, :], preferred_element_type=F32))
    x1_ref[...] = x1
    xn = _rms(x1, gffn_ref[...]).astype(BF16)
    xn_ref[...] = xn
    qp = jnp.dot(xn, wq_ref[...], preferred_element_type=F32).astype(BF16)
    groups = st_ref.shape[4]
    for hp in range(2 * PEER_HEADS):
        s = lax.dot_general(
            keys_ref[hp], qp[:, hp * PEER_HALF:(hp + 1) * PEER_HALF], _NT,
            preferred_element_type=F32)
        for g in range(groups):
            st_ref[hp // 2, hp % 2, 0, :, g] = s[:, g * LANES:(g + 1) * LANES].reshape(
                PEER_KEYS // 8, 8, LANES)


def _outproj_call(o, y, x2, w_out, g_ffn, peer_wq, peer_keys, tm):
    t, d = x2.shape
    nq = peer_wq.shape[1]
    tile = lambda i: (i, 0)
    const2 = lambda i: (0, 0)
    keys = peer_keys.reshape(2 * PEER_HEADS, PEER_KEYS, PEER_HALF).astype(BF16)
    assert ROUTE_TOKENS % tm == 0 and t % ROUTE_TOKENS == 0
    steps_per_tile = ROUTE_TOKENS // tm
    return pl.pallas_call(
        _outproj_kernel,
        grid=(t // tm,),
        in_specs=[
            pl.BlockSpec((tm, ATT_WIDTH), tile),
            pl.BlockSpec((tm, GMLP_WIDTH), tile),
            pl.BlockSpec((tm, d), tile),
            pl.BlockSpec((ATT_WIDTH + GMLP_WIDTH, d), const2),
            pl.BlockSpec((1, d), const2),
            pl.BlockSpec((d, nq), const2),
            pl.BlockSpec((2 * PEER_HEADS, PEER_KEYS, PEER_HALF), lambda i: (0, 0, 0)),
        ],
        out_specs=[
            pl.BlockSpec((tm, d), tile),
            pl.BlockSpec((tm, d), tile),
            pl.BlockSpec((PEER_HEADS, 2, 1, PEER_KEYS // 8, tm // LANES, 8, LANES),
                         lambda i: (0, 0, i // steps_per_tile, 0, i % steps_per_tile, 0, 0)),
        ],
        out_shape=[
            jax.ShapeDtypeStruct((t, d), F32),
            jax.ShapeDtypeStruct((t, d), BF16),
            jax.ShapeDtypeStruct((PEER_HEADS, 2, t // ROUTE_TOKENS, PEER_KEYS // 8, 8, 8, LANES), F32),
        ],
        compiler_params=pltpu.CompilerParams(
            dimension_semantics=("arbitrary",), vmem_limit_bytes=VMEM_LIMIT_BYTES),
        name="outproj_scores",
    )(o, y, x2, w_out.astype(BF16), g_ffn.reshape(1, d), peer_wq.astype(BF16), keys)


def _extract_topk(s, ids, k):
    big = jnp.iinfo(jnp.int32).max
    vals, picks = [], []
    for _ in range(k):
        m = jnp.max(s, axis=0, keepdims=True)
        pick = jnp.min(jnp.where(s == m, ids, big), axis=0, keepdims=True)
        vals.append(m)
        picks.append(pick)
        s = jnp.where(ids == pick, -jnp.inf, s)
    return jnp.concatenate(vals, axis=0), jnp.concatenate(picks, axis=0)


def _take_rows(table, sel, n):
    out = jnp.zeros(sel.shape, table.dtype)
    for j in range(n):
        out = jnp.where(sel == j, table[j:j + 1], out)
    return out


def _sort_network(n):
    pairs = []

    def merge(lo, length, r):
        step = 2 * r
        if step < length:
            merge(lo, length, step)
            merge(lo + r, length, step)
            pairs.extend((i, i + r) for i in range(lo + r, lo + length - r, step))
        else:
            pairs.append((lo, lo + r))

    def sort(lo, length):
        if length > 1:
            sort(lo, length // 2)
            sort(lo + length // 2, length // 2)
            merge(lo, length, 1)

    sort(0, n)
    return pairs


def _compare_exchange(vals, ids, i, j):
    a, b = vals[i], vals[j]
    ge = a >= b
    vals[i], vals[j] = jnp.maximum(a, b), jnp.minimum(a, b)
    ids[i], ids[j] = jnp.where(ge, ids[i], ids[j]), jnp.where(ge, ids[j], ids[i])


def _bitonic_merge(vals, ids):
    n = len(vals)
    d = n // 2
    while d:
        for i in range(n):
            if (i // d) % 2 == 0:
                _compare_exchange(vals, ids, i, i + d)
        d //= 2


def _merge_top(av, ai, bv, bi):
    n = len(av)
    cv, ci = [], []
    for r in range(n):
        a, b = av[r], bv[n - 1 - r]
        cv.append(jnp.maximum(a, b))
        ci.append(jnp.where(a >= b, ai[r], bi[n - 1 - r]))
    _bitonic_merge(cv, ci)
    return cv, ci


def _merge_all(av, ai, bv, bi):
    cv, ci = av + bv[::-1], ai + bi[::-1]
    _bitonic_merge(cv, ci)
    return cv, ci


def _tie_flag(rows, top, k):
    flag = jnp.zeros(top[0].shape, I32)
    for r in range(k - 1):
        flag = jnp.where(top[r] == top[r + 1], 1, flag)
    count = jnp.zeros(top[0].shape, I32)
    for row in rows:
        count = count + jnp.where(row >= top[k - 1], 1, 0)
    return jnp.where(count != k, 1, flag)


def _net_topk(rows, k):
    pairs = _sort_network(k)
    lists = []
    for g in range(0, len(rows), k):
        vals = list(rows[g:g + k])
        ids = [jnp.full(rows[0].shape, g + j, I32) for j in range(k)]
        for i, j in pairs:
            _compare_exchange(vals, ids, i, j)
        lists.append((vals, ids))
    while len(lists) > 1:
        lists = [_merge_top(*lists[m], *lists[m + 1]) for m in range(0, len(lists), 2)]
    vals, ids = lists[0]
    return vals, ids, _tie_flag(rows, vals, k)


def _route_kernel(st_ref, a_ref, b_ref, g_ref):
    k = PEER_TOPK
    hk = k // 2
    shape = (8, LANES)

    def key_rows(side):
        return [st_ref[0, side, 0, pl.ds((n // 8) * 64 + n % 8, 8, stride=8), :]
                for n in range(PEER_KEYS)]
    widths = [k] + [hk] * (hk - 1)

    def finish(best, expert):
        e = jnp.exp(best - best[0:1])
        a_ref[0] = expert >> KEY_BITS
        b_ref[0] = expert & (PEER_KEYS - 1)
        g_ref[0] = e / jnp.sum(e, axis=0, keepdims=True)

    v1, i1, f1 = _net_topk(key_rows(0), k)
    v2, i2, f2 = _net_topk(key_rows(1), k)
    base = [i * PEER_KEYS for i in i1]
    groups = [([v1[k1] + v2[k2] for k2 in range(w)], [base[k1] + i2[k2] for k2 in range(w)])
              for k1, w in enumerate(widths)]
    groups.append(([v1[k1] + v2[0] for k1 in range(hk, k)], [base[k1] + i2[0] for k1 in range(hk, k)]))
    cand_rows = [v for g in groups for v in g[0]]
    full = [groups[0]] + [_merge_all(*groups[m], *groups[m + 1]) for m in range(1, len(groups), 2)]
    while len(full) > 1:
        nxt = [_merge_top(*full[m], *full[m + 1]) for m in range(0, len(full) - 1, 2)]
        full = nxt + full[len(full) - len(full) % 2:]
    best, expert = full[0]
    finish(jnp.stack(best), jnp.stack(expert))
    tie1 = jnp.max(jnp.maximum(f1, f2))
    tie2 = jnp.max(_tie_flag(cand_rows, best, k))

    def exact_pairs(xv1, xi1, xv2, xi2):
        flat = jnp.concatenate(
            [k1 * k + lax.broadcasted_iota(I32, (w,) + shape, 0) for k1, w in enumerate(widths)]
            + [(hk + lax.broadcasted_iota(I32, (hk,) + shape, 0)) * k], axis=0)
        cand = jnp.concatenate(
            [xv1[k1:k1 + 1] + xv2[0:w] for k1, w in enumerate(widths)]
            + [xv1[hk:k] + xv2[0:1]], axis=0)
        xbest, pick = _extract_topk(cand, flat, k)
        sel1 = pick // k
        sel2 = pick - sel1 * k
        finish(xbest, _take_rows(xi1, sel1, k) * PEER_KEYS + _take_rows(xi2, sel2, k))

    @pl.when(tie1 > 0)
    def _exact():
        key_ids = lax.broadcasted_iota(I32, (PEER_KEYS,) + shape, 0)
        xv1, xi1 = _extract_topk(jnp.stack(key_rows(0)), key_ids, k)
        xv2, xi2 = _extract_topk(jnp.stack(key_rows(1)), key_ids, k)
        exact_pairs(xv1, xi1, xv2, xi2)

    @pl.when((tie1 == 0) & (tie2 > 0))
    def _exact_pairs_only():
        exact_pairs(jnp.stack(v1), jnp.stack(i1), jnp.stack(v2), jnp.stack(i2))


def _route_call(st):
    tiles = st.shape[2]
    slots = PEER_HEADS * PEER_TOPK
    st = st.reshape(PEER_HEADS, 2, tiles, PEER_KEYS * 8, LANES)
    out_spec = pl.BlockSpec((1, PEER_TOPK, 8, LANES), lambda i, h: (i, h, 0, 0))
    out_sds = lambda dt: jax.ShapeDtypeStruct((tiles, slots, 8, LANES), dt)
    outs = pl.pallas_call(
        _route_kernel,
        grid=(tiles, PEER_HEADS),
        in_specs=[pl.BlockSpec((1, 2, 1, PEER_KEYS * 8, LANES), lambda i, h: (h, 0, i, 0, 0))],
        out_specs=[out_spec] * 3,
        out_shape=[out_sds(I32), out_sds(I32), out_sds(F32)],
        compiler_params=pltpu.CompilerParams(
            dimension_semantics=("arbitrary", "arbitrary"),
            vmem_limit_bytes=VMEM_LIMIT_BYTES),
        name="peer_route",
    )(st)
    return [x.reshape(tiles, slots * 8, LANES) for x in outs]


W_PITCH = 136
BUILD_UNROLL = 64
HI16 = 0xFFFF0000
GELU_K0 = math.sqrt(2.0 / math.pi)
GELU_K1 = 0.044715 * GELU_K0


def _peer_kernel(xn_ref, x1_ref, at_ref, bt_ref, gt_ref, wut_ref, wv_ref, gfin_ref,
                 out_ref, wbuf, acc, hb, zs, a_s, b_s, g_s, *, cw, n_chunks, n_blocks):
    tb = xn_ref.shape[0]
    s = pl.program_id(0)
    blk = s // n_chunks
    c = s % n_chunks
    slots = PEER_HEADS * PEER_TOPK

    @pl.when(s == 0)
    def _init():
        acc[...] = jnp.zeros_like(acc)
        hb[1] = jnp.zeros(hb.shape[1:], hb.dtype)
        wbuf[...] = jnp.zeros_like(wbuf)

    def build_gates():
        g0 = lax.rem(blk, ROUTE_TOKENS // tb) * (tb // LANES)
        for g in range(tb // LANES):
            rows = pl.ds(g * LANES, LANES)
            grp = pl.ds(g0 + g, slots, stride=8)
            a_s[rows, :] = at_ref[0, grp, :].T
            b_s[rows, :] = bt_ref[0, grp, :].T
            g_s[rows, :] = gt_ref[0, grp, :].T
        ids = lax.broadcasted_iota(I32, (PEER_KEYS, slots), 0)

        def gate_bits(t):
            arow = a_s[pl.ds(t, 1), :]
            brow = b_s[pl.ds(t, 1), :]
            grow = g_s[pl.ds(t, 1), :]
            ga = jnp.where(ids == arow, 0.5 * grow, 0.0).astype(BF16)
            ob = jnp.where(ids == brow, 1.0, 0.0).astype(BF16)
            wt = lax.dot_general(ga, ob, _NT, preferred_element_type=F32)
            return lax.bitcast_convert_type(wt, jnp.uint32)

        def token_pairs(i, carry):
            for u in range(BUILD_UNROLL // 2):
                p = i * (BUILD_UNROLL // 2) + u
                packed = (gate_bits(2 * p + 1) & jnp.uint32(HI16)) | (gate_bits(2 * p) >> 16)
                wbuf[pl.ds(pl.multiple_of(p * W_PITCH, 8), PEER_KEYS), :] = packed
            return carry

        lax.fori_loop(0, tb // BUILD_UNROLL, token_pairs, 0)

    cur = lax.rem(s, 2)
    prev = 1 - cur
    cb = lax.rem(c + n_chunks - 1, n_chunks)
    for j in range(cw):
        cols = slice(j * PEER_KEYS, (j + 1) * PEER_KEYS)
        w = wbuf[pl.ds(cb * cw + j, tb // 2, stride=W_PITCH), :]
        half_gate = pltpu.bitcast(w, BF16)
        x = hb[prev, :, cols]
        u = x * (x * x * GELU_K1 + GELU_K0)
        zs[0, :, cols] = (x * jnp.tanh(u) + x) * half_gate
    acc[...] += jnp.dot(zs[0], wv_ref[...], preferred_element_type=F32)
    hb[cur] = jnp.dot(xn_ref[...], wut_ref[...], preferred_element_type=F32).astype(BF16)

    @pl.when(c == 0)
    def _block_boundary():
        out_ref[...] = _rms(x1_ref[...] + acc[...], gfin_ref[...])
        acc[...] = jnp.zeros_like(acc)

        @pl.when(blk < n_blocks)
        def _():
            build_gates()


def _peer_call(xn, x1, a_t, b_t, g_t, peer_wu, peer_wv, g_final, tb, cw):
    t, d = xn.shape
    n_exp = peer_wu.shape[0]
    hk = PEER_HEADS * PEER_TOPK
    wut = peer_wu.astype(BF16).T
    wv = peer_wv.astype(BF16)
    n_chunks = PEER_KEYS // cw
    n_blocks = t // tb
    assert ROUTE_TOKENS % tb == 0
    last = n_blocks - 1
    blk_of = lambda s: jnp.minimum(s // n_chunks, last)
    prev_blk_of = lambda s: jnp.clip((s - 1) // n_chunks, 0, last)
    tab = pl.BlockSpec((1, hk * 8, LANES), lambda s: (blk_of(s) // (ROUTE_TOKENS // tb), 0, 0))
    return pl.pallas_call(
        functools.partial(_peer_kernel, cw=cw, n_chunks=n_chunks, n_blocks=n_blocks),
        grid=(n_blocks * n_chunks + 1,),
        in_specs=[
            pl.BlockSpec((tb, d), lambda s: (blk_of(s), 0)),
            pl.BlockSpec((tb, d), lambda s: (prev_blk_of(s), 0)),
            tab, tab, tab,
            pl.BlockSpec((d, cw * PEER_KEYS), lambda s: (0, s % n_chunks)),
            pl.BlockSpec((cw * PEER_KEYS, d), lambda s: ((s + n_chunks - 1) % n_chunks, 0)),
            pl.BlockSpec((1, d), lambda s: (0, 0)),
        ],
        out_specs=pl.BlockSpec((tb, d), lambda s: (prev_blk_of(s), 0)),
        out_shape=jax.ShapeDtypeStruct((t, d), F32),
        scratch_shapes=[
            pltpu.VMEM((tb // 2 * W_PITCH, PEER_KEYS), jnp.uint32),
            pltpu.VMEM((tb, d), F32),
            pltpu.VMEM((2, tb, cw * PEER_KEYS), BF16),
            pltpu.VMEM((1, tb, cw * PEER_KEYS), BF16),
            pltpu.VMEM((tb, hk), I32),
            pltpu.VMEM((tb, hk), I32),
            pltpu.VMEM((tb, hk), F32),
        ],
        compiler_params=pltpu.CompilerParams(
            dimension_semantics=("arbitrary",),
            vmem_limit_bytes=PEER_VMEM_LIMIT_BYTES),
        name="peer_dense",
    )(xn, x1, a_t, b_t, g_t, wut, wv, g_final.reshape(1, d))


def kernel(x, w_in, lam_q1, lam_k1, lam_q2, lam_k2, g_subln, w_s, b_s, g_gv, g_gout,
           w_out, g_mix, g_ffn, peer_wq, peer_keys, peer_wu, peer_wv, g_final):
    batch, seq, d = x.shape
    depth = w_in.shape[0]
    assert depth == 1, "LAM_INIT and the single pass below assume one layer"
    t = batch * seq
    x2 = x.reshape(t, d)
    l = 0
    q, k, v, y = _inproj_call(x2, g_mix[l], w_in[l], w_s[l], b_s[l], g_gv[l], g_gout[l],
                              tm=min(512, t))
    o = _attn_call(q, k, v, lam_q1[l], lam_k1[l], lam_q2[l], lam_k2[l], g_subln[l],
                   batch, seq, qb=min(256, seq))
    x1, xn, st = _outproj_call(o, y, x2, w_out[l], g_ffn[l], peer_wq[l], peer_keys[l],
                               tm=min(512, t))
    a_t, b_t, g_t = _route_call(st)
    out = _peer_call(xn, x1, a_t, b_t, g_t, peer_wu[l], peer_wv[l], g_final,
                     tb=min(512, t), cw=8)
    return out.reshape(batch, seq, d)
```
